```python
import jax, jax.numpy as jnp
from jax import lax
import numpy as np

D_MODEL = 2048
BATCH = 4
SEQ = 2048
DEPTH = 4
DEC_BATCH = 128
DEC_SEQ = 8
PAST_LEN = 16384
PAGE_SIZE = 128

POOL_WIDTH = D_MODEL // 4
POOL_WINDOWS = (2, 4, 8, 16)
POOL_GROUPS = len(POOL_WINDOWS)
POOL_GROUP_DIM = POOL_WIDTH // POOL_GROUPS
POOL_HIST = max(POOL_WINDOWS) - 1
GMLP_WIDTH = D_MODEL // 2
GMLP_HEADS = 8
GMLP_HEAD_DIM = GMLP_WIDTH // GMLP_HEADS
CHUNK = 128
CONV_WIDTH = D_MODEL // 4
CONV_K = 3
CONV_HIST = CONV_K - 1
MIX_WIDTH = POOL_WIDTH + GMLP_WIDTH + CONV_WIDTH
IN_WIDTH = POOL_WIDTH + 2 * GMLP_WIDTH + 3 * CONV_WIDTH
IN_SPLITS = (POOL_WIDTH, POOL_WIDTH + GMLP_WIDTH, POOL_WIDTH + 2 * GMLP_WIDTH,
             POOL_WIDTH + 2 * GMLP_WIDTH + CONV_WIDTH, POOL_WIDTH + 2 * GMLP_WIDTH + 2 * CONV_WIDTH)
D_FF = 4 * D_MODEL
ALPHA = (2 * DEPTH) ** 0.25
BETA = (8 * DEPTH) ** -0.25
LN_EPS = 1e-5

kernel_name = 'hybrid_pool_gmlp_conv_decoder_step'


def _layernorm(x, g, b):
    xf = x.astype(jnp.float32)
    mu = jnp.mean(xf, axis=-1, keepdims=True)
    var = jnp.mean(jnp.square(xf - mu), axis=-1, keepdims=True)
    y = (xf - mu) * lax.rsqrt(var + LN_EPS) * g.astype(jnp.float32) + b.astype(jnp.float32)
    return y.astype(x.dtype)


def _pool_mixer(p, p_hist, pos0, w_pool, pool_scale):
    L = p.shape[1]
    ext = jnp.concatenate([p_hist, p], axis=1)
    ext32 = ext.astype(jnp.float32)
    csum = jnp.concatenate([jnp.zeros_like(ext32[:, :1]), jnp.cumsum(ext32, axis=1)], axis=1)
    end = csum[:, POOL_HIST + 1:POOL_HIST + 1 + L]
    pos = pos0 + jnp.arange(L)
    means = []
    for g, w in enumerate(POOL_WINDOWS):
        cs = slice(g * POOL_GROUP_DIM, (g + 1) * POOL_GROUP_DIM)
        start = csum[:, POOL_HIST + 1 - w:POOL_HIST + 1 - w + L, cs]
        count = jnp.minimum(w, pos + 1).astype(jnp.float32)[None, :, None]
        means.append((end[..., cs] - start) / count)
    pooled = jnp.concatenate(means, axis=-1).astype(p.dtype)
    d = (pooled - p).reshape(p.shape[0], L, POOL_GROUPS, POOL_GROUP_DIM)
    out = jnp.einsum('blgc,gce->blge', d, w_pool).reshape(p.shape[0], L, POOL_WIDTH)
    return out * pool_scale, ext[:, -POOL_HIST:]


def _spatial_gate(u, v, w_s, b_s):
    B_, L, _ = v.shape
    Lp = -(-L // CHUNK) * CHUNK
    vp = v if Lp == L else jnp.pad(v, ((0, 0), (0, Lp - L), (0, 0)))
    v5 = vp.reshape(B_, Lp // CHUNK, CHUNK, GMLP_HEADS, GMLP_HEAD_DIM)
    mask = jnp.tril(jnp.ones((CHUNK, CHUNK), w_s.dtype))
    mixed = jnp.einsum('hij,bcjhd->bcihd', w_s * mask, v5) + jnp.transpose(b_s)[:, :, None]
    mixed = mixed.reshape(B_, Lp, GMLP_WIDTH)[:, :L]
    return u * mixed


def _short_conv(z, z_hist, w_conv):
    L = z.shape[1]
    ext = jnp.concatenate([z_hist, z], axis=1)
    y = w_conv[0] * ext[:, 0:L]
    for k in range(1, CONV_K):
        y = y + w_conv[k] * ext[:, k:k + L]
    return y, ext[:, -CONV_HIST:]


def _layer(x, pool_hist, conv_hist, pos0, w_in, w_pool, pool_scale, ln_v_g, ln_v_b, w_s, b_s,
           w_conv, w_out, ln1_g, ln1_b, w_ff1, w_ff2, ln2_g, ln2_b):
    proj = jnp.einsum('bld,de->ble', x, w_in)
    p, u, v, gate_b, gate_c, z = jnp.split(proj, IN_SPLITS, axis=-1)
    a_out, pool_new = _pool_mixer(p, pool_hist, pos0, w_pool, pool_scale)
    u = jax.nn.gelu(u)
    v = _layernorm(jax.nn.gelu(v), ln_v_g, ln_v_b)
    b_out = _spatial_gate(u, v, w_s, b_s)
    conv_y, conv_new = _short_conv(gate_c * z, conv_hist, w_conv)
    c_out = gate_b * conv_y
    mix = jnp.concatenate([a_out, b_out, c_out], axis=-1)
    h = _layernorm(ALPHA * x + jnp.einsum('ble,ed->bld', mix, w_out), ln1_g, ln1_b)
    f = jnp.einsum('blf,fd->bld', jnp.square(jax.nn.relu(jnp.einsum('bld,df->blf', h, w_ff1))), w_ff2)
    x = _layernorm(ALPHA * h + f, ln2_g, ln2_b)
    return x, pool_new, conv_new, v


def setup_inputs(seed: int = 0) -> dict:
    key = jax.random.key(seed)
    ks = jax.random.split(key, 20)
    nrm = jax.random.normal
    return {
        'x_prompt': nrm(ks[0], (BATCH, SEQ, D_MODEL), jnp.float32),
        'x_sample': nrm(ks[1], (DEC_BATCH, DEC_SEQ, D_MODEL), jnp.float32),
        'state_pool': nrm(ks[2], (DEPTH, DEC_BATCH, POOL_HIST, POOL_WIDTH), jnp.float32),
        'state_conv': nrm(ks[3], (DEPTH, DEC_BATCH, CONV_HIST, CONV_WIDTH), jnp.float32),
        'w_in': nrm(ks[4], (DEPTH, D_MODEL, IN_WIDTH), jnp.float32) * D_MODEL ** -0.5,
        'w_pool': nrm(ks[5], (DEPTH, POOL_GROUPS, POOL_GROUP_DIM, POOL_GROUP_DIM), jnp.float32) * POOL_GROUP_DIM ** -0.5,
        'pool_scale': 1.0 + 0.1 * nrm(ks[6], (DEPTH, POOL_WIDTH), jnp.float32),
        'ln_v_g': 1.0 + 0.05 * nrm(ks[7], (DEPTH, GMLP_WIDTH), jnp.float32),
        'ln_v_b': 0.02 * nrm(ks[8], (DEPTH, GMLP_WIDTH), jnp.float32),
        'w_s': nrm(ks[9], (DEPTH, GMLP_HEADS, CHUNK, CHUNK), jnp.float32) * CHUNK ** -0.5,
        'b_s': 1.0 + 0.1 * nrm(ks[10], (DEPTH, GMLP_HEADS, CHUNK), jnp.float32),
        'w_conv': nrm(ks[11], (DEPTH, CONV_K, CONV_WIDTH), jnp.float32) * CONV_K ** -0.5,
        'w_out': nrm(ks[12], (DEPTH, MIX_WIDTH, D_MODEL), jnp.float32) * (MIX_WIDTH ** -0.5 * BETA),
        'ln1_g': 1.0 + 0.05 * nrm(ks[13], (DEPTH, D_MODEL), jnp.float32),
        'ln1_b': 0.02 * nrm(ks[14], (DEPTH, D_MODEL), jnp.float32),
        'w_ff1': nrm(ks[15], (DEPTH, D_MODEL, D_FF), jnp.float32) * D_MODEL ** -0.5,
        'w_ff2': nrm(ks[16], (DEPTH, D_FF, D_MODEL), jnp.float32) * (D_FF ** -0.5 * BETA),
        'ln2_g': 1.0 + 0.05 * nrm(ks[17], (DEPTH, D_MODEL), jnp.float32),
        'ln2_b': 0.02 * nrm(ks[18], (DEPTH, D_MODEL), jnp.float32),
    }


def reference(x_prompt, x_sample, state_pool, state_conv, w_in, w_pool, pool_scale, ln_v_g, ln_v_b,
              w_s, b_s, w_conv, w_out, ln1_g, ln1_b, w_ff1, w_ff2, ln2_g, ln2_b):
    xp, xs = x_prompt, x_sample
    nb = xp.shape[0]
    pool_p, conv_p, pool_s, conv_s, chunk_v_s = [], [], [], [], []
    for l in range(DEPTH):
        wl = (w_in[l], w_pool[l], pool_scale[l], ln_v_g[l], ln_v_b[l], w_s[l], b_s[l], w_conv[l],
              w_out[l], ln1_g[l], ln1_b[l], w_ff1[l], w_ff2[l], ln2_g[l], ln2_b[l])
        hist_pool0 = jnp.zeros((nb, POOL_HIST, POOL_WIDTH), xp.dtype)
        hist_conv0 = jnp.zeros((nb, CONV_HIST, CONV_WIDTH), xp.dtype)
        xp, pn, cn, _ = _layer(xp, hist_pool0, hist_conv0, 0, *wl)
        pool_p.append(pn)
        conv_p.append(cn)
        xs, pn, cn, vs = _layer(xs, state_pool[l], state_conv[l], PAST_LEN, *wl)
        pool_s.append(pn)
        conv_s.append(cn)
        chunk_v_s.append(vs)
    return (xp, xs, jnp.stack(pool_p), jnp.stack(conv_p), jnp.stack(pool_s), jnp.stack(conv_s), jnp.stack(chunk_v_s))
```

```python
import functools

import jax
import jax.numpy as jnp
from jax import lax
from jax.experimental import pallas as pl
from jax.experimental.pallas import tpu as pltpu

D_MODEL = 2048
DEPTH = 4
POOL_WIDTH = D_MODEL // 4
POOL_WINDOWS = (2, 4, 8, 16)
POOL_GROUP_DIM = POOL_WIDTH // len(POOL_WINDOWS)
POOL_HIST = max(POOL_WINDOWS) - 1
GMLP_WIDTH = D_MODEL // 2
GMLP_HEADS = 8
GMLP_HEAD_DIM = GMLP_WIDTH // GMLP_HEADS
CHUNK = 128
CONV_WIDTH = D_MODEL // 4
CONV_K = 3
CONV_HIST = CONV_K - 1
D_FF = 4 * D_MODEL
PAST_LEN = 16384
ALPHA = (2 * DEPTH) ** 0.25
LN_EPS = 1e-5

OFF_P = 0
OFF_U = OFF_P + POOL_WIDTH
OFF_V = OFF_U + GMLP_WIDTH
OFF_GB = OFF_V + GMLP_WIDTH
OFF_GC = OFF_GB + CONV_WIDTH
OFF_Z = OFF_GC + CONV_WIDTH
IN_WIDTH = OFF_Z + CONV_WIDTH

POOL_HIST_PAD = 16
CONV_HIST_PAD = 8

V7X_VMEM_LIMIT_BYTES = 60 * 1024 * 1024

MIX_TILE_ROWS = 512
SAMPLE_TILE_ROWS = 256
FFN_TILE_ROWS = 512
FFN_TILE_COLS = 512


def _dot(a, b):
    return jnp.dot(a, b, preferred_element_type=jnp.float32)


def _layernorm(x, g, b):
    mu = jnp.mean(x, axis=-1, keepdims=True)
    xc = x - mu
    var = jnp.mean(xc * xc, axis=-1, keepdims=True)
    return xc * lax.rsqrt(var + LN_EPS) * g + b


def _mixer_kernel(is_sample, tm, *refs):
    if is_sample:
        (x_ref, hp_ref, hc_ref, w_in_ref, w_pool_ref, pscale_ref, lnv_g_ref, lnv_b_ref, ws_ref, bs_ref,
         w_conv_ref, w_out_ref, ln1_g_ref, ln1_b_ref,
         h_ref, p_out_ref, cz_out_ref, v_out_ref) = refs
        nseq = hp_ref.shape[0]
        seq_len = tm // nseq
    else:
        (x_ref, w_in_ref, w_pool_ref, pscale_ref, lnv_g_ref, lnv_b_ref, ws_ref, bs_ref,
         w_conv_ref, w_out_ref, ln1_g_ref, ln1_b_ref,
         h_ref, p_out_ref, cz_out_ref, hp_scr, hc_scr) = refs
        j = pl.program_id(1)

        @pl.when(j == 0)
        def _():
            hp_scr[...] = jnp.zeros_like(hp_scr)
            hc_scr[...] = jnp.zeros_like(hc_scr)

    x = x_ref[...]
    xb = x.astype(jnp.bfloat16)

    def proj(off, width):
        return _dot(xb, w_in_ref[:, off:off + width])

    def with_history(hist, new, hist_rows):
        width = new.shape[-1]
        if is_sample:
            ext = jnp.concatenate([hist, new.reshape(nseq, seq_len, width)], axis=1)
            ext = ext.reshape(nseq * (hist_rows + seq_len), width)

            def take_new(s):
                w = s.shape[-1]
                return s.reshape(nseq, hist_rows + seq_len, w)[:, hist_rows:, :].reshape(tm, w)
        else:
            ext = jnp.concatenate([hist, new], axis=0)

            def take_new(s):
                return s[hist_rows:, :]
        return ext, take_new

    p = proj(OFF_P, POOL_WIDTH)
    hist_p = hp_ref[...] if is_sample else hp_scr[...]
    ext, take_new = with_history(hist_p, p, POOL_HIST_PAD)
    if is_sample:
        pos1 = None
    else:
        pos1 = (j * tm + 1 + lax.broadcasted_iota(jnp.int32, (tm, 1), 0)).astype(jnp.float32)
    s = ext
    a_parts = []
    for g, w in enumerate(POOL_WINDOWS):
        s = s + pltpu.roll(s, w // 2, axis=0)
        lo = g * POOL_GROUP_DIM
        win = take_new(s[:, :POOL_GROUP_DIM])
        if is_sample:
            mean = win * (1.0 / w)
        else:
            mean = win / jnp.minimum(jnp.float32(w), pos1)
        d = mean - p[:, lo:lo + POOL_GROUP_DIM]
        a_parts.append(_dot(d.astype(jnp.bfloat16), w_pool_ref[g]))
        if g + 1 < len(POOL_WINDOWS):
            s = s[:, POOL_GROUP_DIM:]
    a_out = jnp.concatenate(a_parts, axis=1) * pscale_ref[...]

    v = _layernorm(jax.nn.gelu(proj(OFF_V, GMLP_WIDTH)), lnv_g_ref[...], lnv_b_ref[...])
    if is_sample:
        v_out_ref[...] = v
    vb = v.astype(jnp.bfloat16)
    row = lax.broadcasted_iota(jnp.int32, (CHUNK, CHUNK), 0)
    col = lax.broadcasted_iota(jnp.int32, (CHUNK, CHUNK), 1)
    if is_sample:
        mask = (row // seq_len == col // seq_len) & (col <= row)
    else:
        mask = col <= row
    nchunk = tm // CHUNK
    mixed_cols = []
    for hd in range(GMLP_HEADS):
        wm = jnp.where(mask, ws_ref[hd], 0.0).astype(jnp.bfloat16)
        lo = hd * GMLP_HEAD_DIM
        rhs = jnp.concatenate(
            [vb[c * CHUNK:(c + 1) * CHUNK, lo:lo + GMLP_HEAD_DIM] for c in range(nchunk)], axis=1)
        out = _dot(wm, rhs)
        mixed_cols.append(jnp.concatenate(
            [out[:, c * GMLP_HEAD_DIM:(c + 1) * GMLP_HEAD_DIM] for c in range(nchunk)], axis=0))
    mixed = jnp.concatenate(mixed_cols, axis=1)
    bias = jnp.concatenate([bs_ref[...]] * nchunk, axis=0)
    b_out = jax.nn.gelu(proj(OFF_U, GMLP_WIDTH)) * (mixed + bias)

    cz = proj(OFF_GC, CONV_WIDTH) * proj(OFF_Z, CONV_WIDTH)
    hist_c = hc_ref[...] if is_sample else hc_scr[...]
    ext_c, take_new_c = with_history(hist_c, cz, CONV_HIST_PAD)
    y = w_conv_ref[CONV_K - 1:CONV_K, :] * ext_c
    for k in range(CONV_K - 1):
        y = y + w_conv_ref[k:k + 1, :] * pltpu.roll(ext_c, CONV_K - 1 - k, axis=0)
    c_out = proj(OFF_GB, CONV_WIDTH) * take_new_c(y)

    mix = jnp.concatenate([a_out, b_out, c_out], axis=1).astype(jnp.bfloat16)
    h_ref[...] = _layernorm(ALPHA * x + _dot(mix, w_out_ref[...]), ln1_g_ref[...], ln1_b_ref[...])

    if is_sample:
        p_out_ref[...] = p
        cz_out_ref[...] = cz
    else:
        hp_new = p[tm - POOL_HIST_PAD:, :]
        hc_new = cz[tm - CONV_HIST_PAD:, :]
        hp_scr[...] = hp_new
        hc_scr[...] = hc_new
        p_out_ref[...] = hp_new
        cz_out_ref[...] = hc_new


def _const_spec(shape):
    nd = len(shape)
    return pl.BlockSpec(shape, lambda *_: (0,) * nd, pipeline_mode=pl.Buffered(1))


def _mixer_weight_specs():
    return [
        _const_spec((D_MODEL, IN_WIDTH)),
        _const_spec((len(POOL_WINDOWS), POOL_GROUP_DIM, POOL_GROUP_DIM)),
        _const_spec((1, POOL_WIDTH)),
        _const_spec((1, GMLP_WIDTH)),
        _const_spec((1, GMLP_WIDTH)),
        _const_spec((GMLP_HEADS, CHUNK, CHUNK)),
        _const_spec((CHUNK, GMLP_WIDTH)),
        _const_spec((CONV_K, CONV_WIDTH)),
        _const_spec((D_MODEL, D_MODEL)),
        _const_spec((1, D_MODEL)),
        _const_spec((1, D_MODEL)),
    ]


def _mixer_prompt(x, weights, batch, seq):
    tm = MIX_TILE_ROWS
    nj = seq // tm
    row_spec = lambda width: pl.BlockSpec((tm, width), lambda b, j: (b * nj + j, 0))
    return pl.pallas_call(
        functools.partial(_mixer_kernel, False, tm),
        grid=(batch, nj),
        in_specs=[row_spec(D_MODEL)] + _mixer_weight_specs(),
        out_specs=[
            row_spec(D_MODEL),
            pl.BlockSpec((None, POOL_HIST_PAD, POOL_WIDTH), lambda b, j: (b, 0, 0)),
            pl.BlockSpec((None, CONV_HIST_PAD, CONV_WIDTH), lambda b, j: (b, 0, 0)),
        ],
        out_shape=[
            jax.ShapeDtypeStruct((batch * seq, D_MODEL), jnp.float32),
            jax.ShapeDtypeStruct((batch, POOL_HIST_PAD, POOL_WIDTH), jnp.float32),
            jax.ShapeDtypeStruct((batch, CONV_HIST_PAD, CONV_WIDTH), jnp.float32),
        ],
        scratch_shapes=[
            pltpu.VMEM((POOL_HIST_PAD, POOL_WIDTH), jnp.float32),
            pltpu.VMEM((CONV_HIST_PAD, CONV_WIDTH), jnp.float32),
        ],
        compiler_params=pltpu.CompilerParams(
            dimension_semantics=("arbitrary", "arbitrary"), vmem_limit_bytes=V7X_VMEM_LIMIT_BYTES),
        name="mixer_prompt",
    )(x, *weights)


def _mixer_sample(x, hist_p, hist_c, weights):
    m = x.shape[0]
    tm = SAMPLE_TILE_ROWS
    nseq = hist_p.shape[0] * tm // m
    rows = lambda width: pl.BlockSpec((tm, width), lambda i: (i, 0))
    seqs = lambda hist, width: pl.BlockSpec((nseq, hist, width), lambda i: (i, 0, 0))
    return pl.pallas_call(
        functools.partial(_mixer_kernel, True, tm),
        grid=(m // tm,),
        in_specs=[rows(D_MODEL), seqs(POOL_HIST_PAD, POOL_WIDTH),
                  seqs(CONV_HIST_PAD, CONV_WIDTH)] + _mixer_weight_specs(),
        out_specs=[rows(D_MODEL), rows(POOL_WIDTH), rows(CONV_WIDTH), rows(GMLP_WIDTH)],
        out_shape=[
            jax.ShapeDtypeStruct((m, D_MODEL), jnp.float32),
            jax.ShapeDtypeStruct((m, POOL_WIDTH), jnp.float32),
            jax.ShapeDtypeStruct((m, CONV_WIDTH), jnp.float32),
            jax.ShapeDtypeStruct((m, GMLP_WIDTH), jnp.float32),
        ],
        compiler_params=pltpu.CompilerParams(
            dimension_semantics=("arbitrary",), vmem_limit_bytes=V7X_VMEM_LIMIT_BYTES),
        name="mixer_sample",
    )(x, hist_p, hist_c, *weights)


def _ffn_kernel(h_ref, w1_ref, w2_ref, g_ref, b_ref, o_ref, hb_scr):
    f = pl.program_id(1)

    @pl.when(f == 0)
    def _():
        hb_scr[...] = h_ref[...].astype(jnp.bfloat16)
        o_ref[...] = jnp.zeros_like(o_ref)

    a = jnp.square(jnp.maximum(_dot(hb_scr[...], w1_ref[...]), 0.0))
    o_ref[...] += _dot(a.astype(jnp.bfloat16), w2_ref[...])

    @pl.when(f == pl.num_programs(1) - 1)
    def _():
        o_ref[...] = _layernorm(ALPHA * h_ref[...] + o_ref[...], g_ref[...], b_ref[...])


def _ffn(h, w1, w2, g, b):
    m = h.shape[0]
    tm, tf = FFN_TILE_ROWS, FFN_TILE_COLS
    return pl.pallas_call(
        _ffn_kernel,
        grid=(m // tm, D_FF // tf),
        in_specs=[
            pl.BlockSpec((tm, D_MODEL), lambda i, f: (i, 0)),
            pl.BlockSpec((D_MODEL, tf), lambda i, f: (0, f)),
            pl.BlockSpec((tf, D_MODEL), lambda i, f: (f, 0)),
            pl.BlockSpec((1, D_MODEL), lambda i, f: (0, 0)),
            pl.BlockSpec((1, D_MODEL), lambda i, f: (0, 0)),
        ],
        out_specs=pl.BlockSpec((tm, D_MODEL), lambda i, f: (i, 0)),
        out_shape=jax.ShapeDtypeStruct((m, D_MODEL), jnp.float32),
        scratch_shapes=[pltpu.VMEM((tm, D_MODEL), jnp.bfloat16)],
        compiler_params=pltpu.CompilerParams(
            dimension_semantics=("arbitrary", "arbitrary"), vmem_limit_bytes=V7X_VMEM_LIMIT_BYTES),
        name="ffn",
    )(h, w1, w2, g, b)


def kernel(x_prompt, x_sample, state_pool, state_conv, w_in, w_pool, pool_scale, ln_v_g, ln_v_b, w_s, b_s, w_conv, w_out, ln1_g, ln1_b, w_ff1, w_ff2, ln2_g, ln2_b):
    bf16 = jnp.bfloat16
    batch, seq, _ = x_prompt.shape
    nseq, dec_len, _ = x_sample.shape
    assert seq % MIX_TILE_ROWS == 0 and MIX_TILE_ROWS % CHUNK == 0
    assert CHUNK % dec_len == 0 and PAST_LEN % CHUNK == 0
    assert (nseq * dec_len) % SAMPLE_TILE_ROWS == 0 and SAMPLE_TILE_ROWS % CHUNK == 0

    w_in_b, w_pool_b, w_out_b = w_in.astype(bf16), w_pool.astype(bf16), w_out.astype(bf16)
    w_ff1_b, w_ff2_b = w_ff1.astype(bf16), w_ff2.astype(bf16)
    row = lambda a: a.reshape(DEPTH, 1, -1)
    pool_scale, ln_v_g, ln_v_b = row(pool_scale), row(ln_v_g), row(ln_v_b)
    ln1_g, ln1_b, ln2_g, ln2_b = row(ln1_g), row(ln1_b), row(ln2_g), row(ln2_b)

    rep = CHUNK // dec_len
    ws_prompt = w_s
    bs_prompt = jnp.repeat(jnp.swapaxes(b_s, 1, 2), GMLP_HEAD_DIM, axis=2)
    ws_sample = jnp.tile(w_s[:, :, :dec_len, :dec_len], (1, 1, rep, rep))
    bs_sample = jnp.repeat(jnp.tile(jnp.swapaxes(b_s[:, :, :dec_len], 1, 2), (1, rep, 1)), GMLP_HEAD_DIM, axis=2)

    hist_p = jnp.pad(state_pool, ((0, 0), (0, 0), (POOL_HIST_PAD - POOL_HIST, 0), (0, 0)))
    hist_c = jnp.pad(state_conv, ((0, 0), (0, 0), (CONV_HIST_PAD - CONV_HIST, 0), (0, 0)))

    xp = x_prompt.reshape(batch * seq, D_MODEL)
    xs = x_sample.reshape(nseq * dec_len, D_MODEL)
    pool_p, conv_p, p_new, cz_new, chunk_v = [], [], [], [], []
    for l in range(DEPTH):
        shared = (w_in_b[l], w_pool_b[l], pool_scale[l], ln_v_g[l], ln_v_b[l])
        tail = (w_conv[l], w_out_b[l], ln1_g[l], ln1_b[l])
        hp, p16, cz8 = _mixer_prompt(xp, shared + (ws_prompt[l], bs_prompt[l]) + tail, batch, seq)
        xp = _ffn(hp, w_ff1_b[l], w_ff2_b[l], ln2_g[l], ln2_b[l])
        pool_p.append(p16[:, POOL_HIST_PAD - POOL_HIST:])
        conv_p.append(cz8[:, CONV_HIST_PAD - CONV_HIST:])
        hs, p_s, cz_s, v_s = _mixer_sample(xs, hist_p[l], hist_c[l], shared + (ws_sample[l], bs_sample[l]) + tail)
        xs = _ffn(hs, w_ff1_b[l], w_ff2_b[l], ln2_g[l], ln2_b[l])
        p_new.append(p_s.reshape(nseq, dec_len, POOL_WIDTH))
        cz_new.append(cz_s.reshape(nseq, dec_len, CONV_WIDTH))
        chunk_v.append(v_s.reshape(nseq, dec_len, GMLP_WIDTH))

    pool_s = jnp.concatenate([state_pool, jnp.stack(p_new)], axis=2)[:, :, -POOL_HIST:]
    conv_s = jnp.concatenate([state_conv, jnp.stack(cz_new)], axis=2)[:, :, -CONV_HIST:]
    return (xp.reshape(batch, seq, D_MODEL), xs.reshape(nseq, dec_len, D_MODEL),
            jnp.stack(pool_p), jnp.stack(conv_p), pool_s, conv_s, jnp.stack(chunk_v))
```

```python
import functools

import jax
import jax.numpy as jnp
from jax import lax
from jax.experimental import pallas as pl
from jax.experimental.pallas import tpu as pltpu

D_MODEL = 2048
DEPTH = 4
POOL_WIDTH = D_MODEL // 4
POOL_WINDOWS = (2, 4, 8, 16)
POOL_GROUP_DIM = POOL_WIDTH // len(POOL_WINDOWS)
POOL_HIST = max(POOL_WINDOWS) - 1
GMLP_WIDTH = D_MODEL // 2
GMLP_HEADS = 8
GMLP_HEAD_DIM = GMLP_WIDTH // GMLP_HEADS
CHUNK = 128
CONV_WIDTH = D_MODEL // 4
CONV_K = 3
CONV_HIST = CONV_K - 1
D_FF = 4 * D_MODEL
PAST_LEN = 16384
ALPHA = (2 * DEPTH) ** 0.25
LN_EPS = 1e-5

OFF_P = 0
OFF_U = OFF_P + POOL_WIDTH
OFF_V = OFF_U + GMLP_WIDTH
OFF_GB = OFF_V + GMLP_WIDTH
OFF_GC = OFF_GB + CONV_WIDTH
OFF_Z = OFF_GC + CONV_WIDTH
IN_WIDTH = OFF_Z + CONV_WIDTH

POOL_HIST_PAD = 16
CONV_HIST_PAD = 8

V7X_VMEM_LIMIT_BYTES = 60 * 1024 * 1024

MIX_TILE_ROWS = 512
SAMPLE_TILE_ROWS = 256
FFN_TILE_ROWS = 1024
FFN_TILE_COLS = 512


def _dot(a, b):
    return jnp.dot(a, b, preferred_element_type=jnp.float32)


def _layernorm(x, g, b):
    mu = jnp.mean(x, axis=-1, keepdims=True)
    xc = x - mu
    var = jnp.mean(xc * xc, axis=-1, keepdims=True)
    return xc * lax.rsqrt(var + LN_EPS) * g + b


def _mixer_kernel(is_sample, tm, *refs):
    if is_sample:
        (x_ref, hp_ref, hc_ref, w_in_ref, w_pool_ref, pscale_ref, lnv_g_ref, lnv_b_ref, ws_ref, bs_ref,
         w_conv_ref, w_out_ref, ln1_g_ref, ln1_b_ref,
         h_ref, p_out_ref, cz_out_ref, v_out_ref) = refs
        nseq = hp_ref.shape[0]
        seq_len = tm // nseq
    else:
        (x_ref, w_in_ref, w_pool_ref, pscale_ref, lnv_g_ref, lnv_b_ref, ws_ref, bs_ref,
         w_conv_ref, w_out_ref, ln1_g_ref, ln1_b_ref,
         h_ref, p_out_ref, cz_out_ref, hp_scr, hc_scr) = refs
        j = pl.program_id(1)

        @pl.when(j == 0)
        def _():
            hp_scr[...] = jnp.zeros_like(hp_scr)
            hc_scr[...] = jnp.zeros_like(hc_scr)

    x = x_ref[...]
    xb = x.astype(jnp.bfloat16)

    def proj(off, width):
        return _dot(xb, w_in_ref[:, off:off + width])

    def with_history(hist, new, hist_rows):
        width = new.shape[-1]
        if is_sample:
            ext = jnp.concatenate([hist, new.reshape(nseq, seq_len, width)], axis=1)
            ext = ext.reshape(nseq * (hist_rows + seq_len), width)

            def take_new(s):
                w = s.shape[-1]
                return s.reshape(nseq, hist_rows + seq_len, w)[:, hist_rows:, :].reshape(tm, w)
        else:
            ext = jnp.concatenate([hist, new], axis=0)

            def take_new(s):
                return s[hist_rows:, :]
        return ext, take_new

    p = proj(OFF_P, POOL_WIDTH)
    hist_p = hp_ref[...] if is_sample else hp_scr[...]
    ext, take_new = with_history(hist_p, p, POOL_HIST_PAD)
    if is_sample:
        pos1 = None
    else:
        pos1 = (j * tm + 1 + lax.broadcasted_iota(jnp.int32, (tm, 1), 0)).astype(jnp.float32)
    s = ext
    a_parts = []
    for g, w in enumerate(POOL_WINDOWS):
        s = s + pltpu.roll(s, w // 2, axis=0)
        lo = g * POOL_GROUP_DIM
        win = take_new(s[:, :POOL_GROUP_DIM])
        if is_sample:
            mean = win * (1.0 / w)
        else:
            mean = win / jnp.minimum(jnp.float32(w), pos1)
        d = mean - p[:, lo:lo + POOL_GROUP_DIM]
        a_parts.append(_dot(d.astype(jnp.bfloat16), w_pool_ref[g]))
        if g + 1 < len(POOL_WINDOWS):
            s = s[:, POOL_GROUP_DIM:]
    a_out = jnp.concatenate(a_parts, axis=1) * pscale_ref[...]

    v = _layernorm(jax.nn.gelu(proj(OFF_V, GMLP_WIDTH)), lnv_g_ref[...], lnv_b_ref[...])
    if is_sample:
        v_out_ref[...] = v
    vb = v.astype(jnp.bfloat16)
    row = lax.broadcasted_iota(jnp.int32, (CHUNK, CHUNK), 0)
    col = lax.broadcasted_iota(jnp.int32, (CHUNK, CHUNK), 1)
    if is_sample:
        mask = (row // seq_len == col // seq_len) & (col <= row)
    else:
        mask = col <= row
    nchunk = tm // CHUNK
    mixed_cols = []
    for hd in range(GMLP_HEADS):
        wm = jnp.where(mask, ws_ref[hd], 0.0).astype(jnp.bfloat16)
        lo = hd * GMLP_HEAD_DIM
        rhs = jnp.concatenate(
            [vb[c * CHUNK:(c + 1) * CHUNK, lo:lo + GMLP_HEAD_DIM] for c in range(nchunk)], axis=1)
        out = _dot(wm, rhs)
        mixed_cols.append(jnp.concatenate(
            [out[:, c * GMLP_HEAD_DIM:(c + 1) * GMLP_HEAD_DIM] for c in range(nchunk)], axis=0))
    mixed = jnp.concatenate(mixed_cols, axis=1)
    bias = jnp.concatenate([bs_ref[...]] * nchunk, axis=0)
    b_out = jax.nn.gelu(proj(OFF_U, GMLP_WIDTH)) * (mixed + bias)

    cz = proj(OFF_GC, CONV_WIDTH) * proj(OFF_Z, CONV_WIDTH)
    hist_c = hc_ref[...] if is_sample else hc_scr[...]
    ext_c, take_new_c = with_history(hist_c, cz, CONV_HIST_PAD)
    y = w_conv_ref[CONV_K - 1:CONV_K, :] * ext_c
    for k in range(CONV_K - 1):
        y = y + w_conv_ref[k:k + 1, :] * pltpu.roll(ext_c, CONV_K - 1 - k, axis=0)
    c_out = proj(OFF_GB, CONV_WIDTH) * take_new_c(y)

    mix = jnp.concatenate([a_out, b_out, c_out], axis=1).astype(jnp.bfloat16)
    h_ref[...] = _layernorm(ALPHA * x + _dot(mix, w_out_ref[...]), ln1_g_ref[...], ln1_b_ref[...])

    if is_sample:
        p_out_ref[...] = p
        cz_out_ref[...] = cz
    else:
        hp_new = p[tm - POOL_HIST_PAD:, :]
        hc_new = cz[tm - CONV_HIST_PAD:, :]
        hp_scr[...] = hp_new
        hc_scr[...] = hc_new
        p_out_ref[...] = hp_new
        cz_out_ref[...] = hc_new


def _layer_spec(layer, shape):
    nd = len(shape)
    return pl.BlockSpec((None,) + shape, lambda *_: (layer,) + (0,) * nd, pipeline_mode=pl.Buffered(1))


def _mixer_weight_specs(layer):
    spec = functools.partial(_layer_spec, layer)
    return [
        spec((D_MODEL, IN_WIDTH)),
        spec((len(POOL_WINDOWS), POOL_GROUP_DIM, POOL_GROUP_DIM)),
        spec((1, POOL_WIDTH)),
        spec((1, GMLP_WIDTH)),
        spec((1, GMLP_WIDTH)),
        spec((GMLP_HEADS, CHUNK, CHUNK)),
        spec((CHUNK, GMLP_WIDTH)),
        spec((CONV_K, CONV_WIDTH)),
        spec((D_MODEL, D_MODEL)),
        spec((1, D_MODEL)),
        spec((1, D_MODEL)),
    ]


def _mixer_prompt(layer, x, weights, batch, seq):
    tm = MIX_TILE_ROWS
    nj = seq // tm
    row_spec = lambda width: pl.BlockSpec((tm, width), lambda b, j: (b * nj + j, 0))
    return pl.pallas_call(
        functools.partial(_mixer_kernel, False, tm),
        grid=(batch, nj),
        in_specs=[row_spec(D_MODEL)] + _mixer_weight_specs(layer),
        out_specs=[
            row_spec(D_MODEL),
            pl.BlockSpec((None, POOL_HIST_PAD, POOL_WIDTH), lambda b, j: (b, 0, 0)),
            pl.BlockSpec((None, CONV_HIST_PAD, CONV_WIDTH), lambda b, j: (b, 0, 0)),
        ],
        out_shape=[
            jax.ShapeDtypeStruct((batch * seq, D_MODEL), jnp.float32),
            jax.ShapeDtypeStruct((batch, POOL_HIST_PAD, POOL_WIDTH), jnp.float32),
            jax.ShapeDtypeStruct((batch, CONV_HIST_PAD, CONV_WIDTH), jnp.float32),
        ],
        scratch_shapes=[
            pltpu.VMEM((POOL_HIST_PAD, POOL_WIDTH), jnp.float32),
            pltpu.VMEM((CONV_HIST_PAD, CONV_WIDTH), jnp.float32),
        ],
        compiler_params=pltpu.CompilerParams(
            dimension_semantics=("arbitrary", "arbitrary"), vmem_limit_bytes=V7X_VMEM_LIMIT_BYTES),
        name="mixer_prompt",
    )(x, *weights)


def _mixer_sample(layer, x, hist_p, hist_c, weights):
    m = x.shape[0]
    tm = SAMPLE_TILE_ROWS
    nseq = hist_p.shape[1] * tm // m
    rows = lambda width: pl.BlockSpec((tm, width), lambda i: (i, 0))
    seqs = lambda hist, width: pl.BlockSpec((None, nseq, hist, width), lambda i: (layer, i, 0, 0))
    return pl.pallas_call(
        functools.partial(_mixer_kernel, True, tm),
        grid=(m // tm,),
        in_specs=[rows(D_MODEL), seqs(POOL_HIST_PAD, POOL_WIDTH),
                  seqs(CONV_HIST_PAD, CONV_WIDTH)] + _mixer_weight_specs(layer),
        out_specs=[rows(D_MODEL), rows(POOL_WIDTH), rows(CONV_WIDTH), rows(GMLP_WIDTH)],
        out_shape=[
            jax.ShapeDtypeStruct((m, D_MODEL), jnp.float32),
            jax.ShapeDtypeStruct((m, POOL_WIDTH), jnp.float32),
            jax.ShapeDtypeStruct((m, CONV_WIDTH), jnp.float32),
            jax.ShapeDtypeStruct((m, GMLP_WIDTH), jnp.float32),
        ],
        compiler_params=pltpu.CompilerParams(
            dimension_semantics=("arbitrary",), vmem_limit_bytes=V7X_VMEM_LIMIT_BYTES),
        name="mixer_sample",
    )(x, hist_p, hist_c, *weights)


def _ffn_kernel(h_hbm, w1_ref, w2_ref, g_ref, b_ref, o_ref, h_scr, hb_scr, h_sem):
    i, f = pl.program_id(0), pl.program_id(1)
    tm = h_scr.shape[0]

    def h_copy(tile):
        return pltpu.make_async_copy(h_hbm.at[pl.ds(tile * tm, tm), :], h_scr, h_sem)

    @pl.when(f == 0)
    def _():
        @pl.when(i == 0)
        def _():
            h_copy(0).start()

        h_copy(i).wait()
        h = h_scr[...]
        hb_scr[...] = h.astype(jnp.bfloat16)
        o_ref[...] = ALPHA * h

        @pl.when(i + 1 < pl.num_programs(0))
        def _():
            h_copy(i + 1).start()

    w1 = w1_ref[...].astype(jnp.bfloat16)
    w2 = w2_ref[...].astype(jnp.bfloat16)
    a = jnp.square(jnp.maximum(_dot(hb_scr[...], w1), 0.0))
    o_ref[...] += _dot(a.astype(jnp.bfloat16), w2)

    @pl.when(f == pl.num_programs(1) - 1)
    def _():
        o_ref[...] = _layernorm(o_ref[...], g_ref[...], b_ref[...])


def _ffn(layer, h, w1, w2, g, b):
    m = h.shape[0]
    tm, tf = FFN_TILE_ROWS, FFN_TILE_COLS
    return pl.pallas_call(
        _ffn_kernel,
        grid=(m // tm, D_FF // tf),
        in_specs=[
            pl.BlockSpec(memory_space=pl.ANY),
            pl.BlockSpec((None, D_MODEL, tf), lambda i, f: (layer, 0, f)),
            pl.BlockSpec((None, tf, D_MODEL), lambda i, f: (layer, f, 0)),
            pl.BlockSpec((None, 1, D_MODEL), lambda i, f: (layer, 0, 0)),
            pl.BlockSpec((None, 1, D_MODEL), lambda i, f: (layer, 0, 0)),
        ],
        out_specs=pl.BlockSpec((tm, D_MODEL), lambda i, f: (i, 0)),
        out_shape=jax.ShapeDtypeStruct((m, D_MODEL), jnp.float32),
        scratch_shapes=[pltpu.VMEM((tm, D_MODEL), jnp.float32), pltpu.VMEM((tm, D_MODEL), jnp.bfloat16),
                        pltpu.SemaphoreType.DMA(())],
        compiler_params=pltpu.CompilerParams(
            dimension_semantics=("arbitrary", "arbitrary"), vmem_limit_bytes=V7X_VMEM_LIMIT_BYTES),
        name="ffn",
    )(h, w1, w2, g, b)


def kernel(x_prompt, x_sample, state_pool, state_conv, w_in, w_pool, pool_scale, ln_v_g, ln_v_b, w_s, b_s, w_conv, w_out, ln1_g, ln1_b, w_ff1, w_ff2, ln2_g, ln2_b):
    bf16 = jnp.bfloat16
    batch, seq, _ = x_prompt.shape
    nseq, dec_len, _ = x_sample.shape
    assert seq % MIX_TILE_ROWS == 0 and MIX_TILE_ROWS % CHUNK == 0
    assert CHUNK % dec_len == 0 and PAST_LEN % CHUNK == 0
    assert (nseq * dec_len) % SAMPLE_TILE_ROWS == 0 and SAMPLE_TILE_ROWS % CHUNK == 0
    assert (batch * seq) % FFN_TILE_ROWS == 0 and (nseq * dec_len) % FFN_TILE_ROWS == 0

    w_in_b, w_pool_b, w_out_b = w_in.astype(bf16), w_pool.astype(bf16), w_out.astype(bf16)
    row = lambda a: a.reshape(DEPTH, 1, -1)
    pool_scale, ln_v_g, ln_v_b = row(pool_scale), row(ln_v_g), row(ln_v_b)
    ln1_g, ln1_b, ln2_g, ln2_b = row(ln1_g), row(ln1_b), row(ln2_g), row(ln2_b)

    rep = CHUNK // dec_len
    bs_prompt = jnp.repeat(jnp.swapaxes(b_s, 1, 2), GMLP_HEAD_DIM, axis=2)
    ws_sample = jnp.tile(w_s[:, :, :dec_len, :dec_len], (1, 1, rep, rep))
    bs_sample = jnp.repeat(jnp.tile(jnp.swapaxes(b_s[:, :, :dec_len], 1, 2), (1, rep, 1)), GMLP_HEAD_DIM, axis=2)

    hist_p = jnp.pad(state_pool, ((0, 0), (0, 0), (POOL_HIST_PAD - POOL_HIST, 0), (0, 0)))
    hist_c = jnp.pad(state_conv, ((0, 0), (0, 0), (CONV_HIST_PAD - CONV_HIST, 0), (0, 0)))

    shared = (w_in_b, w_pool_b, pool_scale, ln_v_g, ln_v_b)
    tail = (w_conv, w_out_b, ln1_g, ln1_b)
    w_prompt = shared + (w_s, bs_prompt) + tail
    w_sample = shared + (ws_sample, bs_sample) + tail

    xp = x_prompt.reshape(batch * seq, D_MODEL)
    xs = x_sample.reshape(nseq * dec_len, D_MODEL)
    pool_p, conv_p, p_new, cz_new, chunk_v = [], [], [], [], []
    for l in range(DEPTH):
        hp, p16, cz8 = _mixer_prompt(l, xp, w_prompt, batch, seq)
        xp = _ffn(l, hp, w_ff1, w_ff2, ln2_g, ln2_b)
        pool_p.append(p16[:, POOL_HIST_PAD - POOL_HIST:])
        conv_p.append(cz8[:, CONV_HIST_PAD - CONV_HIST:])
        hs, p_s, cz_s, v_s = _mixer_sample(l, xs, hist_p, hist_c, w_sample)
        xs = _ffn(l, hs, w_ff1, w_ff2, ln2_g, ln2_b)
        p_new.append(p_s.reshape(nseq, dec_len, POOL_WIDTH))
        cz_new.append(cz_s.reshape(nseq, dec_len, CONV_WIDTH))
        chunk_v.append(v_s.reshape(nseq, dec_len, GMLP_WIDTH))

    pool_s = jnp.concatenate([state_pool, jnp.stack(p_new)], axis=2)[:, :, -POOL_HIST:]
    conv_s = jnp.concatenate([state_conv, jnp.stack(cz_new)], axis=2)[:, :, -CONV_HIST:]
    return (xp.reshape(batch, seq, D_MODEL), xs.reshape(nseq, dec_len, D_MODEL),
            jnp.stack(pool_p), jnp.stack(conv_p), pool_s, conv_s, jnp.stack(chunk_v))
```

```python
import functools

import jax
import jax.numpy as jnp
from jax import lax
from jax.experimental import pallas as pl
from jax.experimental.pallas import tpu as pltpu

D_MODEL = 2048
DEPTH = 4
POOL_WIDTH = D_MODEL // 4
POOL_WINDOWS = (2, 4, 8, 16)
POOL_GROUP_DIM = POOL_WIDTH // len(POOL_WINDOWS)
POOL_HIST = max(POOL_WINDOWS) - 1
GMLP_WIDTH = D_MODEL // 2
GMLP_HEADS = 8
GMLP_HEAD_DIM = GMLP_WIDTH // GMLP_HEADS
CHUNK = 128
CONV_WIDTH = D_MODEL // 4
CONV_K = 3
CONV_HIST = CONV_K - 1
D_FF = 4 * D_MODEL
PAST_LEN = 16384
ALPHA = (2 * DEPTH) ** 0.25
LN_EPS = 1e-5

OFF_P = 0
OFF_U = OFF_P + POOL_WIDTH
OFF_V = OFF_U + GMLP_WIDTH
OFF_GB = OFF_V + GMLP_WIDTH
OFF_GC = OFF_GB + CONV_WIDTH
OFF_Z = OFF_GC + CONV_WIDTH
IN_WIDTH = OFF_Z + CONV_WIDTH

POOL_HIST_PAD = 16
CONV_HIST_PAD = 8

V7X_VMEM_LIMIT_BYTES = 60 * 1024 * 1024

MIX_TILE_ROWS = 512
SAMPLE_TILE_ROWS = 256
FFN_TILE_ROWS = 1024
FFN_TILE_COLS = 512


def _dot(a, b):
    return jnp.dot(a, b, preferred_element_type=jnp.float32)


def _layernorm(x, g, b):
    mu = jnp.mean(x, axis=-1, keepdims=True)
    xc = x - mu
    var = jnp.mean(xc * xc, axis=-1, keepdims=True)
    return xc * lax.rsqrt(var + LN_EPS) * g + b


def _mixer_kernel(is_sample, tm, *refs):
    if is_sample:
        (x_ref, hp_ref, hc_ref, w_in_ref, w_pool_ref, pscale_ref, lnv_g_ref, lnv_b_ref, ws_ref, bs_ref,
         w_conv_ref, w_out_ref,
         r_ref, p_out_ref, cz_out_ref, v_out_ref) = refs
        nseq = hp_ref.shape[0]
        seq_len = tm // nseq
    else:
        (x_ref, w_in_ref, w_pool_ref, pscale_ref, lnv_g_ref, lnv_b_ref, ws_ref, bs_ref,
         w_conv_ref, w_out_ref,
         r_ref, p_out_ref, cz_out_ref, hp_scr, hc_scr) = refs
        j = pl.program_id(1)

        @pl.when(j == 0)
        def _():
            hp_scr[...] = jnp.zeros_like(hp_scr)
            hc_scr[...] = jnp.zeros_like(hc_scr)

    x = x_ref[...]
    xb = x.astype(jnp.bfloat16)

    def proj(off, width):
        return _dot(xb, w_in_ref[:, off:off + width])

    def with_history(hist, new, hist_rows):
        width = new.shape[-1]
        if is_sample:
            ext = jnp.concatenate([hist, new.reshape(nseq, seq_len, width)], axis=1)
            ext = ext.reshape(nseq * (hist_rows + seq_len), width)

            def take_new(s):
                w = s.shape[-1]
                return s.reshape(nseq, hist_rows + seq_len, w)[:, hist_rows:, :].reshape(tm, w)
        else:
            ext = jnp.concatenate([hist, new], axis=0)

            def take_new(s):
                return s[hist_rows:, :]
        return ext, take_new

    p = proj(OFF_P, POOL_WIDTH)
    hist_p = hp_ref[...] if is_sample else hp_scr[...]
    ext, take_new = with_history(hist_p, p, POOL_HIST_PAD)
    if is_sample:
        pos1 = None
    else:
        pos1 = (j * tm + 1 + lax.broadcasted_iota(jnp.int32, (tm, 1), 0)).astype(jnp.float32)
    s = ext
    a_parts = []
    for g, w in enumerate(POOL_WINDOWS):
        s = s + pltpu.roll(s, w // 2, axis=0)
        lo = g * POOL_GROUP_DIM
        win = take_new(s[:, :POOL_GROUP_DIM])
        if is_sample:
            mean = win * (1.0 / w)
        else:
            mean = win / jnp.minimum(jnp.float32(w), pos1)
        d = mean - p[:, lo:lo + POOL_GROUP_DIM]
        a_parts.append(_dot(d.astype(jnp.bfloat16), w_pool_ref[g]))
        if g + 1 < len(POOL_WINDOWS):
            s = s[:, POOL_GROUP_DIM:]
    a_out = jnp.concatenate(a_parts, axis=1) * pscale_ref[...]

    v = _layernorm(jax.nn.gelu(proj(OFF_V, GMLP_WIDTH)), lnv_g_ref[...], lnv_b_ref[...])
    if is_sample:
        v_out_ref[...] = v
    vb = v.astype(jnp.bfloat16)
    row = lax.broadcasted_iota(jnp.int32, (CHUNK, CHUNK), 0)
    col = lax.broadcasted_iota(jnp.int32, (CHUNK, CHUNK), 1)
    if is_sample:
        mask = (row // seq_len == col // seq_len) & (col <= row)
    else:
        mask = col <= row
    nchunk = tm // CHUNK
    mixed_cols = []
    for hd in range(GMLP_HEADS):
        wm = jnp.where(mask, ws_ref[hd], 0.0).astype(jnp.bfloat16)
        lo = hd * GMLP_HEAD_DIM
        rhs = jnp.concatenate(
            [vb[c * CHUNK:(c + 1) * CHUNK, lo:lo + GMLP_HEAD_DIM] for c in range(nchunk)], axis=1)
        out = _dot(wm, rhs)
        mixed_cols.append(jnp.concatenate(
            [out[:, c * GMLP_HEAD_DIM:(c + 1) * GMLP_HEAD_DIM] for c in range(nchunk)], axis=0))
    mixed = jnp.concatenate(mixed_cols, axis=1)
    bias = jnp.concatenate([bs_ref[...]] * nchunk, axis=0)
    b_out = jax.nn.gelu(proj(OFF_U, GMLP_WIDTH)) * (mixed + bias)

    cz = proj(OFF_GC, CONV_WIDTH) * proj(OFF_Z, CONV_WIDTH)
    hist_c = hc_ref[...] if is_sample else hc_scr[...]
    ext_c, take_new_c = with_history(hist_c, cz, CONV_HIST_PAD)
    y = w_conv_ref[CONV_K - 1:CONV_K, :] * ext_c
    for k in range(CONV_K - 1):
        y = y + w_conv_ref[k:k + 1, :] * pltpu.roll(ext_c, CONV_K - 1 - k, axis=0)
    c_out = proj(OFF_GB, CONV_WIDTH) * take_new_c(y)

    mix = jnp.concatenate([a_out, b_out, c_out], axis=1).astype(jnp.bfloat16)
    r_ref[...] = ALPHA * x + _dot(mix, w_out_ref[...])

    if is_sample:
        p_out_ref[...] = p
        cz_out_ref[...] = cz
    else:
        hp_new = p[tm - POOL_HIST_PAD:, :]
        hc_new = cz[tm - CONV_HIST_PAD:, :]
        hp_scr[...] = hp_new
        hc_scr[...] = hc_new
        p_out_ref[...] = hp_new
        cz_out_ref[...] = hc_new


def _layer_spec(layer, shape):
    nd = len(shape)
    return pl.BlockSpec((None,) + shape, lambda *_: (layer,) + (0,) * nd, pipeline_mode=pl.Buffered(1))


def _mixer_weight_specs(layer):
    spec = functools.partial(_layer_spec, layer)
    return [
        spec((D_MODEL, IN_WIDTH)),
        spec((len(POOL_WINDOWS), POOL_GROUP_DIM, POOL_GROUP_DIM)),
        spec((1, POOL_WIDTH)),
        spec((1, GMLP_WIDTH)),
        spec((1, GMLP_WIDTH)),
        spec((GMLP_HEADS, CHUNK, CHUNK)),
        spec((CHUNK, GMLP_WIDTH)),
        spec((CONV_K, CONV_WIDTH)),
        spec((D_MODEL, D_MODEL)),
    ]


def _mixer_prompt(layer, x, weights, batch, seq):
    tm = MIX_TILE_ROWS
    nj = seq // tm
    row_spec = lambda width: pl.BlockSpec((tm, width), lambda b, j: (b * nj + j, 0))
    return pl.pallas_call(
        functools.partial(_mixer_kernel, False, tm),
        grid=(batch, nj),
        in_specs=[row_spec(D_MODEL)] + _mixer_weight_specs(layer),
        out_specs=[
            row_spec(D_MODEL),
            pl.BlockSpec((None, POOL_HIST_PAD, POOL_WIDTH), lambda b, j: (b, 0, 0)),
            pl.BlockSpec((None, CONV_HIST_PAD, CONV_WIDTH), lambda b, j: (b, 0, 0)),
        ],
        out_shape=[
            jax.ShapeDtypeStruct((batch * seq, D_MODEL), jnp.float32),
            jax.ShapeDtypeStruct((batch, POOL_HIST_PAD, POOL_WIDTH), jnp.float32),
            jax.ShapeDtypeStruct((batch, CONV_HIST_PAD, CONV_WIDTH), jnp.float32),
        ],
        scratch_shapes=[
            pltpu.VMEM((POOL_HIST_PAD, POOL_WIDTH), jnp.float32),
            pltpu.VMEM((CONV_HIST_PAD, CONV_WIDTH), jnp.float32),
        ],
        compiler_params=pltpu.CompilerParams(
            dimension_semantics=("arbitrary", "arbitrary"), vmem_limit_bytes=V7X_VMEM_LIMIT_BYTES),
        name="mixer_prompt",
    )(x, *weights)


def _mixer_sample(layer, x, hist_p, hist_c, weights):
    m = x.shape[0]
    tm = SAMPLE_TILE_ROWS
    nseq = hist_p.shape[1] * tm // m
    rows = lambda width: pl.BlockSpec((tm, width), lambda i: (i, 0))
    seqs = lambda hist, width: pl.BlockSpec((None, nseq, hist, width), lambda i: (layer, i, 0, 0))
    return pl.pallas_call(
        functools.partial(_mixer_kernel, True, tm),
        grid=(m // tm,),
        in_specs=[rows(D_MODEL), seqs(POOL_HIST_PAD, POOL_WIDTH),
                  seqs(CONV_HIST_PAD, CONV_WIDTH)] + _mixer_weight_specs(layer),
        out_specs=[rows(D_MODEL), rows(POOL_WIDTH), rows(CONV_WIDTH), rows(GMLP_WIDTH)],
        out_shape=[
            jax.ShapeDtypeStruct((m, D_MODEL), jnp.float32),
            jax.ShapeDtypeStruct((m, POOL_WIDTH), jnp.float32),
            jax.ShapeDtypeStruct((m, CONV_WIDTH), jnp.float32),
            jax.ShapeDtypeStruct((m, GMLP_WIDTH), jnp.float32),
        ],
        compiler_params=pltpu.CompilerParams(
            dimension_semantics=("arbitrary",), vmem_limit_bytes=V7X_VMEM_LIMIT_BYTES),
        name="mixer_sample",
    )(x, hist_p, hist_c, *weights)


def _ffn_kernel(n_p, n_s, rp_hbm, rp_ref, rs_ref, w1_ref, w2_ref, g1_ref, b1_ref, g2_ref, b2_ref,
                yp_ref, ys_ref, acc_even, acc_odd, hb_even, hb_odd, sem):
    i, f = pl.program_id(0), pl.program_id(1)
    n = n_p + n_s
    tm = acc_even.shape[0]
    piece = rp_ref.shape[0]
    rows = pl.ds(pl.multiple_of(f * piece, piece), piece)

    def seed(acc_ref, hb_ref, dst_rows, r):
        h = _layernorm(r, g1_ref[...], b1_ref[...])
        hb_ref[dst_rows, :] = h.astype(jnp.bfloat16)
        acc_ref[dst_rows, :] = ALPHA * h

    def finished_piece(acc_ref):
        return _layernorm(acc_ref[rows, :], g2_ref[...], b2_ref[...])

    def step(acc_cur, hb_cur, acc_oth, hb_oth):
        w1 = w1_ref[...].astype(jnp.bfloat16)
        w2 = w2_ref[...].astype(jnp.bfloat16)
        a = jnp.square(jnp.maximum(_dot(hb_cur[...], w1), 0.0))
        acc_cur[...] += _dot(a.astype(jnp.bfloat16), w2)

        y = finished_piece(acc_oth)
        ys_ref[...] = y
        seed(acc_oth, hb_oth, rows, jnp.where(i + 1 < n_p, rp_ref[...], rs_ref[...]))

        @pl.when(i <= n_p)
        def _():
            yp_ref[...] = y

    @pl.when((i == 0) & (f == 0))
    def _():
        first = pltpu.make_async_copy(rp_hbm.at[pl.ds(0, tm), :], acc_even, sem)
        first.start()
        first.wait()
        seed(acc_even, hb_even, slice(None), acc_even[...])
        acc_odd[...] = jnp.zeros_like(acc_odd)

    @pl.when((i < n) & (i % 2 == 0))
    def _():
        step(acc_even, hb_even, acc_odd, hb_odd)

    @pl.when((i < n) & (i % 2 == 1))
    def _():
        step(acc_odd, hb_odd, acc_even, hb_even)

    @pl.when(i == n)
    def _():
        ys_ref[...] = finished_piece(acc_odd if n % 2 == 0 else acc_even)


def _ffn(layer, r_p, r_s, w1, w2, g1, b1, g2, b2):
    tm, tf = FFN_TILE_ROWS, FFN_TILE_COLS
    n_p, n_s = r_p.shape[0] // tm, r_s.shape[0] // tm
    nf = D_FF // tf
    piece = tm // nf

    def piece_spec(first_tile, n_tiles, lag):
        def index(i, f):
            return (jnp.clip((i + lag - first_tile) * nf + f, 0, n_tiles * nf - 1), 0)
        return pl.BlockSpec((piece, D_MODEL), index)

    vec = pl.BlockSpec((None, 1, D_MODEL), lambda i, f: (layer, 0, 0))
    last = n_p + n_s - 1
    return pl.pallas_call(
        functools.partial(_ffn_kernel, n_p, n_s),
        grid=(n_p + n_s + 1, nf),
        in_specs=[
            pl.BlockSpec(memory_space=pl.ANY),
            piece_spec(0, n_p, 1),
            piece_spec(n_p, n_s, 1),
            pl.BlockSpec((None, D_MODEL, tf), lambda i, f: (layer, 0, jnp.where(i > last, nf - 1, f))),
            pl.BlockSpec((None, tf, D_MODEL), lambda i, f: (layer, jnp.where(i > last, nf - 1, f), 0)),
            vec, vec, vec, vec,
        ],
        out_specs=[piece_spec(0, n_p, -1), piece_spec(n_p, n_s, -1)],
        out_shape=[jax.ShapeDtypeStruct(r_p.shape, jnp.float32), jax.ShapeDtypeStruct(r_s.shape, jnp.float32)],
        scratch_shapes=[pltpu.VMEM((tm, D_MODEL), jnp.float32), pltpu.VMEM((tm, D_MODEL), jnp.float32),
                        pltpu.VMEM((tm, D_MODEL), jnp.bfloat16), pltpu.VMEM((tm, D_MODEL), jnp.bfloat16),
                        pltpu.SemaphoreType.DMA(())],
        compiler_params=pltpu.CompilerParams(
            dimension_semantics=("arbitrary", "arbitrary"), vmem_limit_bytes=V7X_VMEM_LIMIT_BYTES),
        name="ffn",
    )(r_p, r_p, r_s, w1, w2, g1, b1, g2, b2)


def kernel(x_prompt, x_sample, state_pool, state_conv, w_in, w_pool, pool_scale, ln_v_g, ln_v_b, w_s, b_s, w_conv, w_out, ln1_g, ln1_b, w_ff1, w_ff2, ln2_g, ln2_b):
    bf16 = jnp.bfloat16
    batch, seq, _ = x_prompt.shape
    nseq, dec_len, _ = x_sample.shape
    assert seq % MIX_TILE_ROWS == 0 and MIX_TILE_ROWS % CHUNK == 0
    assert CHUNK % dec_len == 0 and PAST_LEN % CHUNK == 0
    assert (nseq * dec_len) % SAMPLE_TILE_ROWS == 0 and SAMPLE_TILE_ROWS % CHUNK == 0
    assert (batch * seq) % FFN_TILE_ROWS == 0 and (nseq * dec_len) % FFN_TILE_ROWS == 0

    w_in_b, w_pool_b, w_out_b = w_in.astype(bf16), w_pool.astype(bf16), w_out.astype(bf16)
    row = lambda a: a.reshape(DEPTH, 1, -1)
    pool_scale, ln_v_g, ln_v_b = row(pool_scale), row(ln_v_g), row(ln_v_b)
    ln1_g, ln1_b, ln2_g, ln2_b = row(ln1_g), row(ln1_b), row(ln2_g), row(ln2_b)

    exact = lax.Precision.HIGHEST
    head_cols = (jnp.arange(GMLP_WIDTH)[None, :] // GMLP_HEAD_DIM == jnp.arange(GMLP_HEADS)[:, None]).astype(jnp.float32)
    pos_rows = (jnp.arange(CHUNK)[:, None] % dec_len == jnp.arange(dec_len)[None, :]).astype(jnp.float32)
    bs_prompt = jnp.einsum('lhi,hc->lic', b_s, head_cols, precision=exact)
    bs_sample = jnp.einsum('ik,lhk,hc->lic', pos_rows, b_s[:, :, :dec_len], head_cols, precision=exact)
    ws_sample = jnp.einsum('ik,lhkm,jm->lhij', pos_rows, w_s[:, :, :dec_len, :dec_len], pos_rows, precision=exact)

    hist_p = jnp.pad(state_pool, ((0, 0), (0, 0), (POOL_HIST_PAD - POOL_HIST, 0), (0, 0)))
    hist_c = jnp.pad(state_conv, ((0, 0), (0, 0), (CONV_HIST_PAD - CONV_HIST, 0), (0, 0)))

    shared = (w_in_b, w_pool_b, pool_scale, ln_v_g, ln_v_b)
    tail = (w_conv, w_out_b)
    w_prompt = shared + (w_s, bs_prompt) + tail
    w_sample = shared + (ws_sample, bs_sample) + tail

    xp = x_prompt.reshape(batch * seq, D_MODEL)
    xs = x_sample.reshape(nseq * dec_len, D_MODEL)
    pool_p, conv_p, p_new, cz_new, chunk_v = [], [], [], [], []
    for l in range(DEPTH):
        rp, p16, cz8 = _mixer_prompt(l, xp, w_prompt, batch, seq)
        rs, p_s, cz_s, v_s = _mixer_sample(l, xs, hist_p, hist_c, w_sample)
        xp, xs = _ffn(l, rp, rs, w_ff1, w_ff2, ln1_g, ln1_b, ln2_g, ln2_b)
        pool_p.append(p16[:, POOL_HIST_PAD - POOL_HIST:])
        conv_p.append(cz8[:, CONV_HIST_PAD - CONV_HIST:])
        p_new.append(p_s.reshape(nseq, dec_len, POOL_WIDTH))
        cz_new.append(cz_s.reshape(nseq, dec_len, CONV_WIDTH))
        chunk_v.append(v_s.reshape(nseq, dec_len, GMLP_WIDTH))

    pool_s = jnp.concatenate([state_pool, jnp.stack(p_new)], axis=2)[:, :, -POOL_HIST:]
    conv_s = jnp.concatenate([state_conv, jnp.stack(cz_new)], axis=2)[:, :, -CONV_HIST:]
    return (xp.reshape(batch, seq, D_MODEL), xs.reshape(nseq, dec_len, D_MODEL),
            jnp.stack(pool_p), jnp.stack(conv_p), pool_s, conv_s, jnp.stack(chunk_v))
```

```python
import functools

import jax
import jax.numpy as jnp
from jax import lax
from jax.experimental import pallas as pl
from jax.experimental.pallas import tpu as pltpu

D_MODEL = 2048
DEPTH = 4
POOL_WIDTH = D_MODEL // 4
POOL_WINDOWS = (2, 4, 8, 16)
POOL_GROUP_DIM = POOL_WIDTH // len(POOL_WINDOWS)
POOL_HIST = max(POOL_WINDOWS) - 1
GMLP_WIDTH = D_MODEL // 2
GMLP_HEADS = 8
GMLP_HEAD_DIM = GMLP_WIDTH // GMLP_HEADS
CHUNK = 128
CONV_WIDTH = D_MODEL // 4
CONV_K = 3
CONV_HIST = CONV_K - 1
D_FF = 4 * D_MODEL
PAST_LEN = 16384
ALPHA = (2 * DEPTH) ** 0.25
LN_EPS = 1e-5

OFF_P = 0
OFF_U = OFF_P + POOL_WIDTH
OFF_V = OFF_U + GMLP_WIDTH
OFF_GB = OFF_V + GMLP_WIDTH
OFF_GC = OFF_GB + CONV_WIDTH
OFF_Z = OFF_GC + CONV_WIDTH
IN_WIDTH = OFF_Z + CONV_WIDTH

POOL_HIST_PAD = 16
CONV_HIST_PAD = 8

V7X_VMEM_LIMIT_BYTES = 60 * 1024 * 1024

MIX_TILE_ROWS = 512
SAMPLE_TILE_ROWS = 256
FFN_TILE_ROWS = 1024
FFN_TILE_COLS = 512
CONVERT_STEPS = 128
BF16_SUBLANE_TILE = 16


def _dot(a, b):
    return jnp.dot(a, b, preferred_element_type=jnp.float32)


def _layernorm(x, g, b):
    mu = jnp.mean(x, axis=-1, keepdims=True)
    xc = x - mu
    var = jnp.mean(xc * xc, axis=-1, keepdims=True)
    return xc * lax.rsqrt(var + LN_EPS) * g + b


def _mixer_kernel(is_sample, tm, *refs):
    if is_sample:
        (x_ref, hp_ref, hc_ref, w_in_ref, w_pool_ref, pscale_ref, lnv_g_ref, lnv_b_ref, ws_ref, bs_ref,
         w_conv_ref, w_out_ref,
         r_ref, p_out_ref, cz_out_ref, v_out_ref) = refs
        nseq = hp_ref.shape[0]
        seq_len = tm // nseq
    else:
        (x_ref, w_in_ref, w_pool_ref, pscale_ref, lnv_g_ref, lnv_b_ref, ws_ref, bs_ref,
         w_conv_ref, w_out_ref,
         r_ref, p_out_ref, cz_out_ref, hp_scr, hc_scr) = refs
        j = pl.program_id(1)

        @pl.when(j == 0)
        def _():
            hp_scr[...] = jnp.zeros_like(hp_scr)
            hc_scr[...] = jnp.zeros_like(hc_scr)

    x = x_ref[...]
    xb = x.astype(jnp.bfloat16)

    def proj(off, width):
        return _dot(xb, w_in_ref[:, off:off + width])

    def with_history(hist, new, hist_rows):
        width = new.shape[-1]
        if is_sample:
            ext = jnp.concatenate([hist, new.reshape(nseq, seq_len, width)], axis=1)
            ext = ext.reshape(nseq * (hist_rows + seq_len), width)

            def take_new(s):
                w = s.shape[-1]
                return s.reshape(nseq, hist_rows + seq_len, w)[:, hist_rows:, :].reshape(tm, w)
        else:
            ext = jnp.concatenate([hist, new], axis=0)

            def take_new(s):
                return s[hist_rows:, :]
        return ext, take_new

    v_pre = proj(OFF_V, GMLP_WIDTH)
    p = proj(OFF_P, POOL_WIDTH)
    gate_c = proj(OFF_GC, CONV_WIDTH)
    z = proj(OFF_Z, CONV_WIDTH)
    gate_b = proj(OFF_GB, CONV_WIDTH)
    u_pre = proj(OFF_U, GMLP_WIDTH)

    v = _layernorm(jax.nn.gelu(v_pre), lnv_g_ref[...], lnv_b_ref[...])
    if is_sample:
        v_out_ref[...] = v
    vb = v.astype(jnp.bfloat16)
    row = lax.broadcasted_iota(jnp.int32, (CHUNK, CHUNK), 0)
    col = lax.broadcasted_iota(jnp.int32, (CHUNK, CHUNK), 1)
    if is_sample:
        mask = (row // seq_len == col // seq_len) & (col <= row)
    else:
        mask = col <= row
    nchunk = tm // CHUNK
    last = lax.bitcast_convert_type(u_pre[tm - CHUNK:, GMLP_WIDTH - CHUNK:], jnp.uint32)
    zero_bits = (last >> 16) >> 16
    mixed_cols = []
    for hd in range(GMLP_HEADS):
        wm = jnp.where(mask, ws_ref[hd], 0.0)
        wm = lax.bitcast_convert_type(lax.bitcast_convert_type(wm, jnp.uint32) | zero_bits, jnp.float32)
        wm = wm.astype(jnp.bfloat16)
        lo = hd * GMLP_HEAD_DIM
        rhs = jnp.concatenate(
            [vb[c * CHUNK:(c + 1) * CHUNK, lo:lo + GMLP_HEAD_DIM] for c in range(nchunk)], axis=1)
        out = _dot(wm, rhs)
        mixed_cols.append(jnp.concatenate(
            [out[:, c * GMLP_HEAD_DIM:(c + 1) * GMLP_HEAD_DIM] for c in range(nchunk)], axis=0))
    mixed = jnp.concatenate(mixed_cols, axis=1)
    bias = jnp.concatenate([bs_ref[...]] * nchunk, axis=0)
    b_out = jax.nn.gelu(u_pre) * (mixed + bias)

    hist_p = hp_ref[...] if is_sample else hp_scr[...]
    ext, take_new = with_history(hist_p, p, POOL_HIST_PAD)
    if is_sample:
        pos1 = None
    else:
        pos1 = (j * tm + 1 + lax.broadcasted_iota(jnp.int32, (tm, 1), 0)).astype(jnp.float32)
    s = ext
    a_parts = []
    for g, w in enumerate(POOL_WINDOWS):
        s = s + pltpu.roll(s, w // 2, axis=0)
        lo = g * POOL_GROUP_DIM
        win = take_new(s[:, :POOL_GROUP_DIM])
        if is_sample:
            mean = win * (1.0 / w)
        else:
            mean = win / jnp.minimum(jnp.float32(w), pos1)
        d = mean - p[:, lo:lo + POOL_GROUP_DIM]
        a_parts.append(_dot(d.astype(jnp.bfloat16), w_pool_ref[g]))
        if g + 1 < len(POOL_WINDOWS):
            s = s[:, POOL_GROUP_DIM:]
    a_out = jnp.concatenate(a_parts, axis=1) * pscale_ref[...]

    cz = gate_c * z
    hist_c = hc_ref[...] if is_sample else hc_scr[...]
    ext_c, take_new_c = with_history(hist_c, cz, CONV_HIST_PAD)
    y = w_conv_ref[CONV_K - 1:CONV_K, :] * ext_c
    for k in range(CONV_K - 1):
        y = y + w_conv_ref[k:k + 1, :] * pltpu.roll(ext_c, CONV_K - 1 - k, axis=0)
    c_out = gate_b * take_new_c(y)

    o = _dot(c_out.astype(jnp.bfloat16), w_out_ref[POOL_WIDTH + GMLP_WIDTH:, :])
    o = o + _dot(a_out.astype(jnp.bfloat16), w_out_ref[:POOL_WIDTH, :])
    o = o + _dot(b_out.astype(jnp.bfloat16), w_out_ref[POOL_WIDTH:POOL_WIDTH + GMLP_WIDTH, :])
    r_ref[...] = ALPHA * x + o

    if is_sample:
        p_out_ref[...] = p
        cz_out_ref[...] = cz
    else:
        hp_new = p[tm - POOL_HIST_PAD:, :]
        hc_new = cz[tm - CONV_HIST_PAD:, :]
        hp_scr[...] = hp_new
        hc_scr[...] = hc_new
        p_out_ref[...] = hp_new
        cz_out_ref[...] = hc_new


def _layer_spec(layer, shape):
    nd = len(shape)
    return pl.BlockSpec((None,) + shape, lambda *_: (layer,) + (0,) * nd, pipeline_mode=pl.Buffered(1))


def _mixer_weight_specs(layer, wide_slab):
    spec = functools.partial(_layer_spec, layer)
    return [
        _layer_spec(wide_slab, (D_MODEL, IN_WIDTH)),
        spec((len(POOL_WINDOWS), POOL_GROUP_DIM, POOL_GROUP_DIM)),
        spec((1, POOL_WIDTH)),
        spec((1, GMLP_WIDTH)),
        spec((1, GMLP_WIDTH)),
        spec((GMLP_HEADS, CHUNK, CHUNK)),
        spec((CHUNK, GMLP_WIDTH)),
        spec((CONV_K, CONV_WIDTH)),
        _layer_spec(wide_slab, (D_MODEL, D_MODEL)),
    ]


def _mixer_prompt(layer, wide_slab, x, weights, batch, seq):
    tm = MIX_TILE_ROWS
    nj = seq // tm
    row_spec = lambda width: pl.BlockSpec((tm, width), lambda b, j: (b * nj + j, 0))
    return pl.pallas_call(
        functools.partial(_mixer_kernel, False, tm),
        grid=(batch, nj),
        in_specs=[row_spec(D_MODEL)] + _mixer_weight_specs(layer, wide_slab),
        out_specs=[
            row_spec(D_MODEL),
            pl.BlockSpec((None, POOL_HIST_PAD, POOL_WIDTH), lambda b, j: (b, 0, 0)),
            pl.BlockSpec((None, CONV_HIST_PAD, CONV_WIDTH), lambda b, j: (b, 0, 0)),
        ],
        out_shape=[
            jax.ShapeDtypeStruct((batch * seq, D_MODEL), jnp.float32),
            jax.ShapeDtypeStruct((batch, POOL_HIST_PAD, POOL_WIDTH), jnp.float32),
            jax.ShapeDtypeStruct((batch, CONV_HIST_PAD, CONV_WIDTH), jnp.float32),
        ],
        scratch_shapes=[
            pltpu.VMEM((POOL_HIST_PAD, POOL_WIDTH), jnp.float32),
            pltpu.VMEM((CONV_HIST_PAD, CONV_WIDTH), jnp.float32),
        ],
        compiler_params=pltpu.CompilerParams(
            dimension_semantics=("arbitrary", "arbitrary"), vmem_limit_bytes=V7X_VMEM_LIMIT_BYTES),
        name="mixer_prompt",
    )(x, *weights)


def _mixer_sample(layer, wide_slab, x, hist_p, hist_c, weights):
    m = x.shape[0]
    tm = SAMPLE_TILE_ROWS
    nseq = hist_p.shape[1] * tm // m
    rows = lambda width: pl.BlockSpec((tm, width), lambda i: (i, 0))
    seqs = lambda hist, width: pl.BlockSpec((None, nseq, hist, width), lambda i: (layer, i, 0, 0))
    return pl.pallas_call(
        functools.partial(_mixer_kernel, True, tm),
        grid=(m // tm,),
        in_specs=[rows(D_MODEL), seqs(POOL_HIST_PAD, POOL_WIDTH),
                  seqs(CONV_HIST_PAD, CONV_WIDTH)] + _mixer_weight_specs(layer, wide_slab),
        out_specs=[rows(D_MODEL), rows(POOL_WIDTH), rows(CONV_WIDTH), rows(GMLP_WIDTH)],
        out_shape=[
            jax.ShapeDtypeStruct((m, D_MODEL), jnp.float32),
            jax.ShapeDtypeStruct((m, POOL_WIDTH), jnp.float32),
            jax.ShapeDtypeStruct((m, CONV_WIDTH), jnp.float32),
            jax.ShapeDtypeStruct((m, GMLP_WIDTH), jnp.float32),
        ],
        compiler_params=pltpu.CompilerParams(
            dimension_semantics=("arbitrary",), vmem_limit_bytes=V7X_VMEM_LIMIT_BYTES),
        name="mixer_sample",
    )(x, hist_p, hist_c, *weights)


def _ffn_kernel(n_p, n_s, n_next, rp_hbm, rp_ref, rs_ref, w1_ref, w2_ref, g1_ref, b1_ref, g2_ref, b2_ref, *rest):
    next_f32 = rest[:n_next]
    yp_ref, ys_ref = rest[n_next:n_next + 2]
    next_bf16 = rest[n_next + 2:2 * n_next + 2]
    acc_even, acc_odd, hb_even, hb_odd, sem = rest[2 * n_next + 2:]
    i, f = pl.program_id(0), pl.program_id(1)
    n = n_p + n_s
    tm = acc_even.shape[0]
    piece = rp_ref.shape[0]
    rows = pl.ds(pl.multiple_of(f * piece, piece), piece)

    def seed(acc_ref, hb_ref, dst_rows, r):
        h = _layernorm(r, g1_ref[...], b1_ref[...])
        hb_ref[dst_rows, :] = h.astype(jnp.bfloat16)
        acc_ref[dst_rows, :] = ALPHA * h
        return h

    def finished_piece(acc_ref):
        return _layernorm(acc_ref[rows, :], g2_ref[...], b2_ref[...])

    def zero_after(*values):
        tiles = []
        for v in values:
            bits = lax.bitcast_convert_type(v, jnp.uint32)
            tiles += [bits[r:r + 8, c:c + 128] for r in range(0, v.shape[0], 8) for c in range(0, v.shape[1], 128)]
        word = functools.reduce(lambda p, q: p | q, tiles)
        return lax.bitcast_convert_type((word >> 16) >> 16, jnp.float32)

    def step(acc_cur, hb_cur, acc_oth, hb_oth):
        y = finished_piece(acc_oth)
        yp_ref[...] = y
        h_next = seed(acc_oth, hb_oth, rows, jnp.where(i + 1 < n_p, rp_ref[...], rs_ref[...]))
        for src, dst in zip(next_f32, next_bf16):
            dst[...] = src[...].astype(jnp.bfloat16)

        w1, w2 = w1_ref[...], w2_ref[...]
        if w1.dtype != jnp.bfloat16:
            w1, w2 = w1.astype(jnp.bfloat16), w2.astype(jnp.bfloat16)
        tf = w1.shape[1]
        floor = jnp.concatenate([zero_after(y, h_next)] * (tf // 128), axis=1)
        floor = jnp.broadcast_to(floor[None], (tm // 8, 8, tf)).reshape(tm, tf)
        a = jnp.square(jnp.maximum(_dot(hb_cur[...], w1), floor))
        acc_cur[...] += _dot(a.astype(jnp.bfloat16), w2)

    @pl.when((i == 0) & (f == 0))
    def _():
        first = pltpu.make_async_copy(rp_hbm.at[pl.ds(0, tm), :], acc_even, sem)
        first.start()
        first.wait()
        seed(acc_even, hb_even, slice(None), acc_even[...])
        acc_odd[...] = jnp.zeros_like(acc_odd)

    @pl.when((i < n) & (i % 2 == 0))
    def _():
        step(acc_even, hb_even, acc_odd, hb_odd)

    @pl.when((i < n) & (i % 2 == 1))
    def _():
        step(acc_odd, hb_odd, acc_even, hb_even)

    @pl.when(i == n)
    def _():
        ys_ref[...] = finished_piece(acc_odd if n % 2 == 0 else acc_even)


def _ffn(layer, r_p, r_s, w1, w2, w_slab, g1, b1, g2, b2, next_weights):
    tm, tf = FFN_TILE_ROWS, FFN_TILE_COLS
    n_p, n_s = r_p.shape[0] // tm, r_s.shape[0] // tm
    nf = D_FF // tf
    piece = tm // nf
    last = n_p + n_s - 1

    def piece_spec(first_tile, n_tiles, lag):
        def index(i, f):
            return (jnp.clip((i + lag - first_tile) * nf + f, 0, n_tiles * nf - 1), 0)
        return pl.BlockSpec((piece, D_MODEL), index)

    def slab_index(i, f):
        return jnp.minimum(i * nf + f, CONVERT_STEPS - 1)

    assert n_s == 1
    assert (last + 1) * nf >= CONVERT_STEPS
    next_in, next_out, next_shapes = [], [], []
    for w in next_weights:
        slab_rows = w.shape[1] // CONVERT_STEPS
        assert slab_rows % BF16_SUBLANE_TILE == 0
        next_in.append(pl.BlockSpec((None, slab_rows, w.shape[2]), lambda i, f: (layer + 1, slab_index(i, f), 0)))
        next_out.append(pl.BlockSpec((None, slab_rows, w.shape[2]), lambda i, f: (0, slab_index(i, f), 0)))
        next_shapes.append(jax.ShapeDtypeStruct((1,) + w.shape[1:], jnp.bfloat16))

    vec = pl.BlockSpec((None, 1, D_MODEL), lambda i, f: (layer, 0, 0))
    chunk = lambda i, f: jnp.where(i > last, nf - 1, f)
    return pl.pallas_call(
        functools.partial(_ffn_kernel, n_p, n_s, len(next_weights)),
        grid=(n_p + n_s + 1, nf),
        in_specs=[
            pl.BlockSpec(memory_space=pl.ANY),
            piece_spec(0, n_p, 1),
            piece_spec(n_p, n_s, 1),
            pl.BlockSpec((None, D_MODEL, tf), lambda i, f: (w_slab, 0, chunk(i, f))),
            pl.BlockSpec((None, tf, D_MODEL), lambda i, f: (w_slab, chunk(i, f), 0)),
            vec, vec, vec, vec,
        ] + next_in,
        out_specs=[piece_spec(0, n_p, -1), piece_spec(n_p, n_s, -1)] + next_out,
        out_shape=[jax.ShapeDtypeStruct(r_p.shape, jnp.float32),
                   jax.ShapeDtypeStruct(r_s.shape, jnp.float32)] + next_shapes,
        scratch_shapes=[pltpu.VMEM((tm, D_MODEL), jnp.float32), pltpu.VMEM((tm, D_MODEL), jnp.float32),
                        pltpu.VMEM((tm, D_MODEL), jnp.bfloat16), pltpu.VMEM((tm, D_MODEL), jnp.bfloat16),
                        pltpu.SemaphoreType.DMA(())],
        compiler_params=pltpu.CompilerParams(
            dimension_semantics=("arbitrary", "arbitrary"), vmem_limit_bytes=V7X_VMEM_LIMIT_BYTES),
        name="ffn",
    )(r_p, r_p, r_s, w1, w2, g1, b1, g2, b2, *next_weights)


def kernel(x_prompt, x_sample, state_pool, state_conv, w_in, w_pool, pool_scale, ln_v_g, ln_v_b, w_s, b_s, w_conv, w_out, ln1_g, ln1_b, w_ff1, w_ff2, ln2_g, ln2_b):
    bf16 = jnp.bfloat16
    batch, seq, _ = x_prompt.shape
    nseq, dec_len, _ = x_sample.shape
    assert seq % MIX_TILE_ROWS == 0 and MIX_TILE_ROWS % CHUNK == 0
    assert CHUNK % dec_len == 0 and PAST_LEN % CHUNK == 0
    assert (nseq * dec_len) % SAMPLE_TILE_ROWS == 0 and SAMPLE_TILE_ROWS % CHUNK == 0
    assert (batch * seq) % FFN_TILE_ROWS == 0 and (nseq * dec_len) % FFN_TILE_ROWS == 0

    w_pool_b = w_pool.astype(bf16)
    row = lambda a: a.reshape(DEPTH, 1, -1)
    pool_scale, ln_v_g, ln_v_b = row(pool_scale), row(ln_v_g), row(ln_v_b)
    ln1_g, ln1_b, ln2_g, ln2_b = row(ln1_g), row(ln1_b), row(ln2_g), row(ln2_b)

    exact = lax.Precision.HIGHEST
    head_cols = (jnp.arange(GMLP_WIDTH)[None, :] // GMLP_HEAD_DIM == jnp.arange(GMLP_HEADS)[:, None]).astype(jnp.float32)
    pos_rows = (jnp.arange(CHUNK)[:, None] % dec_len == jnp.arange(dec_len)[None, :]).astype(jnp.float32)
    bs_prompt = jnp.einsum('lhi,hc->lic', b_s, head_cols, precision=exact)
    bs_sample = jnp.einsum('ik,lhk,hc->lic', pos_rows, b_s[:, :, :dec_len], head_cols, precision=exact)
    ws_sample = jnp.einsum('ik,lhkm,jm->lhij', pos_rows, w_s[:, :, :dec_len, :dec_len], pos_rows, precision=exact)

    hist_p = jnp.pad(state_pool, ((0, 0), (0, 0), (POOL_HIST_PAD - POOL_HIST, 0), (0, 0)))
    hist_c = jnp.pad(state_conv, ((0, 0), (0, 0), (CONV_HIST_PAD - CONV_HIST, 0), (0, 0)))

    xp = x_prompt.reshape(batch * seq, D_MODEL)
    xs = x_sample.reshape(nseq * dec_len, D_MODEL)
    pool_p, conv_p, p_new, cz_new, chunk_v = [], [], [], [], []
    w_in_b, w_out_b = w_in[:1].astype(bf16), w_out[:1].astype(bf16)
    w1, w2 = w_ff1, w_ff2
    for l in range(DEPTH):
        shared = (w_in_b, w_pool_b, pool_scale, ln_v_g, ln_v_b)
        tail = (w_conv, w_out_b)
        rp, p16, cz8 = _mixer_prompt(l, 0, xp, shared + (w_s, bs_prompt) + tail, batch, seq)
        rs, p_s, cz_s, v_s = _mixer_sample(l, 0, xs, hist_p, hist_c, shared + (ws_sample, bs_sample) + tail)
        next_weights = (w_in, w_out, w_ff1, w_ff2) if l + 1 < DEPTH else ()
        xp, xs, *rounded = _ffn(l, rp, rs, w1, w2, 0, ln1_g, ln1_b, ln2_g, ln2_b, next_weights)
        if rounded:
            w_in_b, w_out_b, w1, w2 = rounded
        pool_p.append(p16[:, POOL_HIST_PAD - POOL_HIST:])
        conv_p.append(cz8[:, CONV_HIST_PAD - CONV_HIST:])
        p_new.append(p_s.reshape(nseq, dec_len, POOL_WIDTH))
        cz_new.append(cz_s.reshape(nseq, dec_len, CONV_WIDTH))
        chunk_v.append(v_s.reshape(nseq, dec_len, GMLP_WIDTH))

    pool_s = jnp.concatenate([state_pool, jnp.stack(p_new)], axis=2)[:, :, -POOL_HIST:]
    conv_s = jnp.concatenate([state_conv, jnp.stack(cz_new)], axis=2)[:, :, -CONV_HIST:]
    return (xp.reshape(batch, seq, D_MODEL), xs.reshape(nseq, dec_len, D_MODEL),
            jnp.stack(pool_p), jnp.stack(conv_p), pool_s, conv_s, jnp.stack(chunk_v))
```

```python
import functools

import jax
import jax.numpy as jnp
from jax import lax
from jax.experimental import pallas as pl
from jax.experimental.pallas import tpu as pltpu

D_MODEL = 2048
DEPTH = 4
POOL_WIDTH = D_MODEL // 4
POOL_WINDOWS = (2, 4, 8, 16)
POOL_GROUP_DIM = POOL_WIDTH // len(POOL_WINDOWS)
POOL_HIST = max(POOL_WINDOWS) - 1
GMLP_WIDTH = D_MODEL // 2
GMLP_HEADS = 8
GMLP_HEAD_DIM = GMLP_WIDTH // GMLP_HEADS
CHUNK = 128
CONV_WIDTH = D_MODEL // 4
CONV_K = 3
CONV_HIST = CONV_K - 1
D_FF = 4 * D_MODEL
PAST_LEN = 16384
ALPHA = (2 * DEPTH) ** 0.25
LN_EPS = 1e-5

OFF_P = 0
OFF_U = OFF_P + POOL_WIDTH
OFF_V = OFF_U + GMLP_WIDTH
OFF_GB = OFF_V + GMLP_WIDTH
OFF_GC = OFF_GB + CONV_WIDTH
OFF_Z = OFF_GC + CONV_WIDTH
IN_WIDTH = OFF_Z + CONV_WIDTH

POOL_HIST_PAD = 16
CONV_HIST_PAD = 8

V7X_VMEM_LIMIT_BYTES = 60 * 1024 * 1024

MIX_TILE_ROWS = 512
SAMPLE_TILE_ROWS = 256
FFN_TILE_ROWS = 1024
FFN_TILE_COLS = 512
FFN_TILE_COLS_NO_NEXT = 1024
CONVERT_STEPS = 128
BF16_SUBLANE_TILE = 16


def _dot(a, b):
    return jnp.dot(a, b, preferred_element_type=jnp.float32)


def _layernorm(x, g, b):
    mu = jnp.mean(x, axis=-1, keepdims=True)
    xc = x - mu
    var = jnp.mean(xc * xc, axis=-1, keepdims=True)
    return xc * lax.rsqrt(var + LN_EPS) * g + b


def _mixer_kernel(is_sample, tm, *refs):
    if is_sample:
        (x_ref, hp_ref, hc_ref, w_in_ref, w_pool_ref, pscale_ref, lnv_g_ref, lnv_b_ref, ws_ref, bs_ref,
         w_conv_ref, w_out_ref,
         r_ref, p_out_ref, cz_out_ref, v_out_ref) = refs
        nseq = hp_ref.shape[0]
        seq_len = tm // nseq

        def carried_history(ref, state_rows):
            _, pad_rows, width = ref.shape
            flat = ref[...].reshape(nseq * pad_rows, width)
            return pltpu.roll(flat, pad_rows - state_rows, axis=0).reshape(nseq, pad_rows, width)
    else:
        (x_ref, w_in_ref, w_pool_ref, pscale_ref, lnv_g_ref, lnv_b_ref, ws_ref, bs_ref,
         w_conv_ref, w_out_ref,
         r_ref, p_out_ref, cz_out_ref, hp_scr, hc_scr) = refs
        j = pl.program_id(1)

        @pl.when(j == 0)
        def _():
            hp_scr[...] = jnp.zeros_like(hp_scr)
            hc_scr[...] = jnp.zeros_like(hc_scr)

    x = x_ref[...]
    xb = x.astype(jnp.bfloat16)

    def proj(off, width):
        return _dot(xb, w_in_ref[:, off:off + width])

    def with_history(hist, new, hist_rows):
        width = new.shape[-1]
        if is_sample:
            ext = jnp.concatenate([hist, new.reshape(nseq, seq_len, width)], axis=1)
            ext = ext.reshape(nseq * (hist_rows + seq_len), width)

            def take_new(s):
                w = s.shape[-1]
                return s.reshape(nseq, hist_rows + seq_len, w)[:, hist_rows:, :].reshape(tm, w)
        else:
            ext = jnp.concatenate([hist, new], axis=0)

            def take_new(s):
                return s[hist_rows:, :]
        return ext, take_new

    v_pre = proj(OFF_V, GMLP_WIDTH)
    p = proj(OFF_P, POOL_WIDTH)
    gate_c = proj(OFF_GC, CONV_WIDTH)
    z = proj(OFF_Z, CONV_WIDTH)
    gate_b = proj(OFF_GB, CONV_WIDTH)
    u_pre = proj(OFF_U, GMLP_WIDTH)

    v = _layernorm(jax.nn.gelu(v_pre), lnv_g_ref[...], lnv_b_ref[...])
    if is_sample:
        v_out_ref[...] = v
    vb = v.astype(jnp.bfloat16)
    row = lax.broadcasted_iota(jnp.int32, (CHUNK, CHUNK), 0)
    col = lax.broadcasted_iota(jnp.int32, (CHUNK, CHUNK), 1)
    if is_sample:
        mask = (row // seq_len == col // seq_len) & (col <= row)
    else:
        mask = col <= row
    nchunk = tm // CHUNK
    last = lax.bitcast_convert_type(u_pre[tm - CHUNK:, GMLP_WIDTH - CHUNK:], jnp.uint32)
    zero_bits = (last >> 16) >> 16
    mixed_cols = []
    for hd in range(GMLP_HEADS):
        wm = jnp.where(mask, ws_ref[hd], 0.0)
        wm = lax.bitcast_convert_type(lax.bitcast_convert_type(wm, jnp.uint32) | zero_bits, jnp.float32)
        wm = wm.astype(jnp.bfloat16)
        lo = hd * GMLP_HEAD_DIM
        rhs = jnp.concatenate(
            [vb[c * CHUNK:(c + 1) * CHUNK, lo:lo + GMLP_HEAD_DIM] for c in range(nchunk)], axis=1)
        out = _dot(wm, rhs)
        mixed_cols.append(jnp.concatenate(
            [out[:, c * GMLP_HEAD_DIM:(c + 1) * GMLP_HEAD_DIM] for c in range(nchunk)], axis=0))
    mixed = jnp.concatenate(mixed_cols, axis=1)
    bias = jnp.concatenate([bs_ref[...]] * nchunk, axis=0)
    b_out = jax.nn.gelu(u_pre) * (mixed + bias)

    hist_p = carried_history(hp_ref, POOL_HIST) if is_sample else hp_scr[...]
    ext, take_new = with_history(hist_p, p, POOL_HIST_PAD)
    if is_sample:
        pos1 = None
    else:
        pos1 = (j * tm + 1 + lax.broadcasted_iota(jnp.int32, (tm, 1), 0)).astype(jnp.float32)
    s = ext
    a_parts = []
    for g, w in enumerate(POOL_WINDOWS):
        s = s + pltpu.roll(s, w // 2, axis=0)
        lo = g * POOL_GROUP_DIM
        win = take_new(s[:, :POOL_GROUP_DIM])
        if is_sample:
            mean = win * (1.0 / w)
        else:
            mean = win / jnp.minimum(jnp.float32(w), pos1)
        d = mean - p[:, lo:lo + POOL_GROUP_DIM]
        a_parts.append(_dot(d.astype(jnp.bfloat16), w_pool_ref[g]))
        if g + 1 < len(POOL_WINDOWS):
            s = s[:, POOL_GROUP_DIM:]
    a_out = jnp.concatenate(a_parts, axis=1) * pscale_ref[...]

    cz = gate_c * z
    hist_c = carried_history(hc_ref, CONV_HIST) if is_sample else hc_scr[...]
    ext_c, take_new_c = with_history(hist_c, cz, CONV_HIST_PAD)
    y = w_conv_ref[CONV_K - 1:CONV_K, :] * ext_c
    for k in range(CONV_K - 1):
        y = y + w_conv_ref[k:k + 1, :] * pltpu.roll(ext_c, CONV_K - 1 - k, axis=0)
    c_out = gate_b * take_new_c(y)

    o = _dot(c_out.astype(jnp.bfloat16), w_out_ref[POOL_WIDTH + GMLP_WIDTH:, :])
    o = o + _dot(a_out.astype(jnp.bfloat16), w_out_ref[:POOL_WIDTH, :])
    o = o + _dot(b_out.astype(jnp.bfloat16), w_out_ref[POOL_WIDTH:POOL_WIDTH + GMLP_WIDTH, :])
    r_ref[...] = ALPHA * x + o

    if is_sample:
        p_out_ref[...] = p
        cz_out_ref[...] = cz
    else:
        hp_new = p[tm - POOL_HIST_PAD:, :]
        hc_new = cz[tm - CONV_HIST_PAD:, :]
        hp_scr[...] = hp_new
        hc_scr[...] = hc_new
        p_out_ref[...] = hp_new
        cz_out_ref[...] = hc_new


def _layer_spec(layer, shape):
    nd = len(shape)
    return pl.BlockSpec((None,) + shape, lambda *_: (layer,) + (0,) * nd, pipeline_mode=pl.Buffered(1))


def _mixer_weight_specs(layer, wide_slab):
    spec = functools.partial(_layer_spec, layer)
    return [
        _layer_spec(wide_slab, (D_MODEL, IN_WIDTH)),
        spec((len(POOL_WINDOWS), POOL_GROUP_DIM, POOL_GROUP_DIM)),
        spec((1, POOL_WIDTH)),
        spec((1, GMLP_WIDTH)),
        spec((1, GMLP_WIDTH)),
        spec((GMLP_HEADS, CHUNK, CHUNK)),
        spec((CHUNK, GMLP_WIDTH)),
        spec((CONV_K, CONV_WIDTH)),
        _layer_spec(wide_slab, (D_MODEL, D_MODEL)),
    ]


def _mixer_prompt(layer, wide_slab, x, weights, batch, seq):
    tm = MIX_TILE_ROWS
    nj = seq // tm
    row_spec = lambda width: pl.BlockSpec((tm, width), lambda b, j: (b * nj + j, 0))
    return pl.pallas_call(
        functools.partial(_mixer_kernel, False, tm),
        grid=(batch, nj),
        in_specs=[row_spec(D_MODEL)] + _mixer_weight_specs(layer, wide_slab),
        out_specs=[
            row_spec(D_MODEL),
            pl.BlockSpec((None, POOL_HIST_PAD, POOL_WIDTH), lambda b, j: (b, 0, 0)),
            pl.BlockSpec((None, CONV_HIST_PAD, CONV_WIDTH), lambda b, j: (b, 0, 0)),
        ],
        out_shape=[
            jax.ShapeDtypeStruct((batch * seq, D_MODEL), jnp.float32),
            jax.ShapeDtypeStruct((batch, POOL_HIST_PAD, POOL_WIDTH), jnp.float32),
            jax.ShapeDtypeStruct((batch, CONV_HIST_PAD, CONV_WIDTH), jnp.float32),
        ],
        scratch_shapes=[
            pltpu.VMEM((POOL_HIST_PAD, POOL_WIDTH), jnp.float32),
            pltpu.VMEM((CONV_HIST_PAD, CONV_WIDTH), jnp.float32),
        ],
        compiler_params=pltpu.CompilerParams(
            dimension_semantics=("arbitrary", "arbitrary"), vmem_limit_bytes=V7X_VMEM_LIMIT_BYTES),
        name="mixer_prompt",
    )(x, *weights)


def _mixer_sample(layer, wide_slab, x, state_pool, state_conv, weights):
    m = x.shape[0]
    tm = SAMPLE_TILE_ROWS
    nseq = state_pool.shape[1] * tm // m
    rows = lambda width: pl.BlockSpec((tm, width), lambda i: (i, 0))
    seqs = lambda pad, width: pl.BlockSpec((None, nseq, pad, width), lambda i: (layer, i, 0, 0))
    return pl.pallas_call(
        functools.partial(_mixer_kernel, True, tm),
        grid=(m // tm,),
        in_specs=[rows(D_MODEL), seqs(POOL_HIST_PAD, POOL_WIDTH),
                  seqs(CONV_HIST_PAD, CONV_WIDTH)] + _mixer_weight_specs(layer, wide_slab),
        out_specs=[rows(D_MODEL), rows(POOL_WIDTH), rows(CONV_WIDTH), rows(GMLP_WIDTH)],
        out_shape=[
            jax.ShapeDtypeStruct((m, D_MODEL), jnp.float32),
            jax.ShapeDtypeStruct((m, POOL_WIDTH), jnp.float32),
            jax.ShapeDtypeStruct((m, CONV_WIDTH), jnp.float32),
            jax.ShapeDtypeStruct((m, GMLP_WIDTH), jnp.float32),
        ],
        compiler_params=pltpu.CompilerParams(
            dimension_semantics=("arbitrary",), vmem_limit_bytes=V7X_VMEM_LIMIT_BYTES),
        name="mixer_sample",
    )(x, state_pool, state_conv, *weights)


def _ffn_kernel(n_p, n_s, n_next, rp_hbm, rp_ref, rs_ref, w1_ref, w2_ref, g1_ref, b1_ref, g2_ref, b2_ref, *rest):
    next_f32 = rest[:n_next]
    yp_ref, ys_ref = rest[n_next:n_next + 2]
    next_bf16 = rest[n_next + 2:2 * n_next + 2]
    acc_even, acc_odd, hb_even, hb_odd, sem = rest[2 * n_next + 2:]
    i, f = pl.program_id(0), pl.program_id(1)
    n = n_p + n_s
    tm = acc_even.shape[0]
    piece = rp_ref.shape[0]
    rows = pl.ds(pl.multiple_of(f * piece, piece), piece)

    def seed(acc_ref, hb_ref, dst_rows, r):
        h = _layernorm(r, g1_ref[...], b1_ref[...])
        hb_ref[dst_rows, :] = h.astype(jnp.bfloat16)
        acc_ref[dst_rows, :] = ALPHA * h
        return h

    def finished_piece(acc_ref):
        return _layernorm(acc_ref[rows, :], g2_ref[...], b2_ref[...])

    def zero_after(*values):
        tiles = []
        for v in values:
            bits = lax.bitcast_convert_type(v, jnp.uint32)
            tiles += [bits[r:r + 8, c:c + 128] for r in range(0, v.shape[0], 8) for c in range(0, v.shape[1], 128)]
        word = functools.reduce(lambda p, q: p | q, tiles)
        return lax.bitcast_convert_type((word >> 16) >> 16, jnp.float32)

    def step(acc_cur, hb_cur, acc_oth, hb_oth):
        y = finished_piece(acc_oth)
        yp_ref[...] = y
        h_next = seed(acc_oth, hb_oth, rows, jnp.where(i + 1 < n_p, rp_ref[...], rs_ref[...]))
        for src, dst in zip(next_f32, next_bf16):
            dst[...] = src[...].astype(jnp.bfloat16)

        w1, w2 = w1_ref[...], w2_ref[...]
        if w1.dtype != jnp.bfloat16:
            w1, w2 = w1.astype(jnp.bfloat16), w2.astype(jnp.bfloat16)
        tf = w1.shape[1]
        floor = jnp.concatenate([zero_after(y, h_next)] * (tf // 128), axis=1)
        floor = jnp.broadcast_to(floor[None], (tm // 8, 8, tf)).reshape(tm, tf)
        a = jnp.square(jnp.maximum(_dot(hb_cur[...], w1), floor))
        acc_cur[...] += _dot(a.astype(jnp.bfloat16), w2)

    @pl.when((i == 0) & (f == 0))
    def _():
        first = pltpu.make_async_copy(rp_hbm.at[pl.ds(0, tm), :], acc_even, sem)
        first.start()
        first.wait()
        seed(acc_even, hb_even, slice(None), acc_even[...])
        acc_odd[...] = jnp.zeros_like(acc_odd)

    @pl.when((i < n) & (i % 2 == 0))
    def _():
        step(acc_even, hb_even, acc_odd, hb_odd)

    @pl.when((i < n) & (i % 2 == 1))
    def _():
        step(acc_odd, hb_odd, acc_even, hb_even)

    @pl.when(i == n)
    def _():
        ys_ref[...] = finished_piece(acc_odd if n % 2 == 0 else acc_even)


def _ffn(layer, r_p, r_s, w1, w2, w_slab, g1, b1, g2, b2, next_weights):
    tm = FFN_TILE_ROWS
    tf = FFN_TILE_COLS if next_weights else FFN_TILE_COLS_NO_NEXT
    n_p, n_s = r_p.shape[0] // tm, r_s.shape[0] // tm
    nf = D_FF // tf
    piece = tm // nf
    last = n_p + n_s - 1

    def piece_spec(first_tile, n_tiles, lag):
        def index(i, f):
            return (jnp.clip((i + lag - first_tile) * nf + f, 0, n_tiles * nf - 1), 0)
        return pl.BlockSpec((piece, D_MODEL), index)

    def slab_index(i, f):
        return jnp.minimum(i * nf + f, CONVERT_STEPS - 1)

    assert n_s == 1
    assert not next_weights or (last + 1) * nf >= CONVERT_STEPS
    next_in, next_out, next_shapes = [], [], []
    for w in next_weights:
        slab_rows = w.shape[1] // CONVERT_STEPS
        assert slab_rows % BF16_SUBLANE_TILE == 0
        next_in.append(pl.BlockSpec((None, slab_rows, w.shape[2]), lambda i, f: (layer + 1, slab_index(i, f), 0)))
        next_out.append(pl.BlockSpec((None, slab_rows, w.shape[2]), lambda i, f: (0, slab_index(i, f), 0)))
        next_shapes.append(jax.ShapeDtypeStruct((1,) + w.shape[1:], jnp.bfloat16))

    vec = pl.BlockSpec((None, 1, D_MODEL), lambda i, f: (layer, 0, 0))
    chunk = lambda i, f: jnp.where(i > last, nf - 1, f)
    return pl.pallas_call(
        functools.partial(_ffn_kernel, n_p, n_s, len(next_weights)),
        grid=(n_p + n_s + 1, nf),
        in_specs=[
            pl.BlockSpec(memory_space=pl.ANY),
            piece_spec(0, n_p, 1),
            piece_spec(n_p, n_s, 1),
            pl.BlockSpec((None, D_MODEL, tf), lambda i, f: (w_slab, 0, chunk(i, f))),
            pl.BlockSpec((None, tf, D_MODEL), lambda i, f: (w_slab, chunk(i, f), 0)),
            vec, vec, vec, vec,
        ] + next_in,
        out_specs=[piece_spec(0, n_p, -1), piece_spec(n_p, n_s, -1)] + next_out,
        out_shape=[jax.ShapeDtypeStruct(r_p.shape, jnp.float32),
                   jax.ShapeDtypeStruct(r_s.shape, jnp.float32)] + next_shapes,
        scratch_shapes=[pltpu.VMEM((tm, D_MODEL), jnp.float32), pltpu.VMEM((tm, D_MODEL), jnp.float32),
                        pltpu.VMEM((tm, D_MODEL), jnp.bfloat16), pltpu.VMEM((tm, D_MODEL), jnp.bfloat16),
                        pltpu.SemaphoreType.DMA(())],
        compiler_params=pltpu.CompilerParams(
            dimension_semantics=("arbitrary", "arbitrary"), vmem_limit_bytes=V7X_VMEM_LIMIT_BYTES),
        name="ffn",
    )(r_p, r_p, r_s, w1, w2, g1, b1, g2, b2, *next_weights)


def kernel(x_prompt, x_sample, state_pool, state_conv, w_in, w_pool, pool_scale, ln_v_g, ln_v_b, w_s, b_s, w_conv, w_out, ln1_g, ln1_b, w_ff1, w_ff2, ln2_g, ln2_b):
    bf16 = jnp.bfloat16
    batch, seq, _ = x_prompt.shape
    nseq, dec_len, _ = x_sample.shape
    assert seq % MIX_TILE_ROWS == 0 and MIX_TILE_ROWS % CHUNK == 0
    assert CHUNK % dec_len == 0 and PAST_LEN % CHUNK == 0
    assert (nseq * dec_len) % SAMPLE_TILE_ROWS == 0 and SAMPLE_TILE_ROWS % CHUNK == 0
    assert (batch * seq) % FFN_TILE_ROWS == 0 and (nseq * dec_len) % FFN_TILE_ROWS == 0

    w_pool_b = w_pool.astype(bf16)
    row = lambda a: a.reshape(DEPTH, 1, -1)
    pool_scale, ln_v_g, ln_v_b = row(pool_scale), row(ln_v_g), row(ln_v_b)
    ln1_g, ln1_b, ln2_g, ln2_b = row(ln1_g), row(ln1_b), row(ln2_g), row(ln2_b)

    exact = lax.Precision.HIGHEST
    head_cols = (jnp.arange(GMLP_WIDTH)[None, :] // GMLP_HEAD_DIM == jnp.arange(GMLP_HEADS)[:, None]).astype(jnp.float32)
    pos_rows = (jnp.arange(CHUNK)[:, None] % dec_len == jnp.arange(dec_len)[None, :]).astype(jnp.float32)
    bs_prompt = jnp.einsum('lhi,hc->lic', b_s, head_cols, precision=exact)
    bs_sample = jnp.einsum('ik,lhk,hc->lic', pos_rows, b_s[:, :, :dec_len], head_cols, precision=exact)
    ws_sample = jnp.einsum('ik,lhkm,jm->lhij', pos_rows, w_s[:, :, :dec_len, :dec_len], pos_rows, precision=exact)

    xp = x_prompt.reshape(batch * seq, D_MODEL)
    xs = x_sample.reshape(nseq * dec_len, D_MODEL)
    pool_p, conv_p, p_new, cz_new, chunk_v = [], [], [], [], []
    w_in_b, w_out_b = w_in[:1].astype(bf16), w_out[:1].astype(bf16)
    w1, w2 = w_ff1, w_ff2
    for l in range(DEPTH):
        shared = (w_in_b, w_pool_b, pool_scale, ln_v_g, ln_v_b)
        tail = (w_conv, w_out_b)
        rp, p16, cz8 = _mixer_prompt(l, 0, xp, shared + (w_s, bs_prompt) + tail, batch, seq)
        rs, p_s, cz_s, v_s = _mixer_sample(l, 0, xs, state_pool, state_conv,
                                           shared + (ws_sample, bs_sample) + tail)
        next_weights = (w_in, w_out, w_ff1, w_ff2) if l + 1 < DEPTH else ()
        xp, xs, *rounded = _ffn(l, rp, rs, w1, w2, 0, ln1_g, ln1_b, ln2_g, ln2_b, next_weights)
        if rounded:
            w_in_b, w_out_b, w1, w2 = rounded
        pool_p.append(p16[:, POOL_HIST_PAD - POOL_HIST:])
        conv_p.append(cz8[:, CONV_HIST_PAD - CONV_HIST:])
        p_new.append(p_s.reshape(nseq, dec_len, POOL_WIDTH))
        cz_new.append(cz_s.reshape(nseq, dec_len, CONV_WIDTH))
        chunk_v.append(v_s.reshape(nseq, dec_len, GMLP_WIDTH))

    pool_s = jnp.concatenate([state_pool, jnp.stack(p_new)], axis=2)[:, :, -POOL_HIST:]
    conv_s = jnp.concatenate([state_conv, jnp.stack(cz_new)], axis=2)[:, :, -CONV_HIST:]
    return (xp.reshape(batch, seq, D_MODEL), xs.reshape(nseq, dec_len, D_MODEL),
            jnp.stack(pool_p), jnp.stack(conv_p), pool_s, conv_s, jnp.stack(chunk_v))
```

```python
import functools

import jax
import jax.numpy as jnp
from jax import lax
from jax.experimental import pallas as pl
from jax.experimental.pallas import tpu as pltpu

D_MODEL = 2048
DEPTH = 4
POOL_WIDTH = D_MODEL // 4
POOL_WINDOWS = (2, 4, 8, 16)
POOL_GROUP_DIM = POOL_WIDTH // len(POOL_WINDOWS)
POOL_HIST = max(POOL_WINDOWS) - 1
GMLP_WIDTH = D_MODEL // 2
GMLP_HEADS = 8
GMLP_HEAD_DIM = GMLP_WIDTH // GMLP_HEADS
CHUNK = 128
CONV_WIDTH = D_MODEL // 4
CONV_K = 3
CONV_HIST = CONV_K - 1
D_FF = 4 * D_MODEL
PAST_LEN = 16384
ALPHA = (2 * DEPTH) ** 0.25
LN_EPS = 1e-5

OFF_P = 0
OFF_U = OFF_P + POOL_WIDTH
OFF_V = OFF_U + GMLP_WIDTH
OFF_GB = OFF_V + GMLP_WIDTH
OFF_GC = OFF_GB + CONV_WIDTH
OFF_Z = OFF_GC + CONV_WIDTH
IN_WIDTH = OFF_Z + CONV_WIDTH

POOL_HIST_PAD = 16
CONV_HIST_PAD = 8

V7X_VMEM_LIMIT_BYTES = 60 * 1024 * 1024

MIX_TILE_ROWS = 512
SAMPLE_TILE_ROWS = 256
FFN_TILE_ROWS = 1024
FFN_TILE_COLS_F32 = 512
FFN_TILE_COLS_BF16 = 1024
BF16_SUBLANE_TILE = 16


def _dot(a, b):
    return jnp.dot(a, b, preferred_element_type=jnp.float32)


def _layernorm(x, g, b):
    mu = jnp.mean(x, axis=-1, keepdims=True)
    xc = x - mu
    var = jnp.mean(xc * xc, axis=-1, keepdims=True)
    return xc * lax.rsqrt(var + LN_EPS) * g + b


def _mixer_kernel(is_sample, tm, *refs):
    if is_sample:
        (x_ref, hp_ref, hc_ref, w_in_ref, w_pool_ref, pscale_ref, lnv_g_ref, lnv_b_ref, ws_ref, bs_ref,
         w_conv_ref, w_out_ref,
         r_ref, p_out_ref, cz_out_ref, v_out_ref) = refs
        nseq = hp_ref.shape[0]
        seq_len = tm // nseq

        def carried_history(ref, state_rows):
            _, pad_rows, width = ref.shape
            flat = ref[...].reshape(nseq * pad_rows, width)
            return pltpu.roll(flat, pad_rows - state_rows, axis=0).reshape(nseq, pad_rows, width)
    else:
        (x_ref, w_in_ref, w_pool_ref, pscale_ref, lnv_g_ref, lnv_b_ref, ws_ref, bs_ref,
         w_conv_ref, w_out_ref,
         r_ref, p_out_ref, cz_out_ref, hp_scr, hc_scr) = refs
        j = pl.program_id(1)

        @pl.when(j == 0)
        def _():
            hp_scr[...] = jnp.zeros_like(hp_scr)
            hc_scr[...] = jnp.zeros_like(hc_scr)

    x = x_ref[...]
    xb = x.astype(jnp.bfloat16)

    def proj(off, width):
        return _dot(xb, w_in_ref[:, off:off + width])

    def with_history(hist, new, hist_rows):
        width = new.shape[-1]
        if is_sample:
            ext = jnp.concatenate([hist, new.reshape(nseq, seq_len, width)], axis=1)
            ext = ext.reshape(nseq * (hist_rows + seq_len), width)

            def take_new(s):
                w = s.shape[-1]
                return s.reshape(nseq, hist_rows + seq_len, w)[:, hist_rows:, :].reshape(tm, w)
        else:
            ext = jnp.concatenate([hist, new], axis=0)

            def take_new(s):
                return s[hist_rows:, :]
        return ext, take_new

    v_pre = proj(OFF_V, GMLP_WIDTH)
    p = proj(OFF_P, POOL_WIDTH)
    gate_c = proj(OFF_GC, CONV_WIDTH)
    z = proj(OFF_Z, CONV_WIDTH)
    gate_b = proj(OFF_GB, CONV_WIDTH)
    u_pre = proj(OFF_U, GMLP_WIDTH)

    v = _layernorm(jax.nn.gelu(v_pre), lnv_g_ref[...], lnv_b_ref[...])
    if is_sample:
        v_out_ref[...] = v
    vb = v.astype(jnp.bfloat16)
    row = lax.broadcasted_iota(jnp.int32, (CHUNK, CHUNK), 0)
    col = lax.broadcasted_iota(jnp.int32, (CHUNK, CHUNK), 1)
    if is_sample:
        mask = (row // seq_len == col // seq_len) & (col <= row)
    else:
        mask = col <= row
    nchunk = tm // CHUNK
    last = lax.bitcast_convert_type(u_pre[tm - CHUNK:, GMLP_WIDTH - CHUNK:], jnp.uint32)
    zero_bits = (last >> 16) >> 16
    mixed_cols = []
    for hd in range(GMLP_HEADS):
        wm = jnp.where(mask, ws_ref[hd], 0.0)
        wm = lax.bitcast_convert_type(lax.bitcast_convert_type(wm, jnp.uint32) | zero_bits, jnp.float32)
        wm = wm.astype(jnp.bfloat16)
        lo = hd * GMLP_HEAD_DIM
        rhs = jnp.concatenate(
            [vb[c * CHUNK:(c + 1) * CHUNK, lo:lo + GMLP_HEAD_DIM] for c in range(nchunk)], axis=1)
        out = _dot(wm, rhs)
        mixed_cols.append(jnp.concatenate(
            [out[:, c * GMLP_HEAD_DIM:(c + 1) * GMLP_HEAD_DIM] for c in range(nchunk)], axis=0))
    mixed = jnp.concatenate(mixed_cols, axis=1)
    bias = jnp.concatenate([bs_ref[...]] * nchunk, axis=0)
    b_out = jax.nn.gelu(u_pre) * (mixed + bias)

    hist_p = carried_history(hp_ref, POOL_HIST) if is_sample else hp_scr[...]
    ext, take_new = with_history(hist_p, p, POOL_HIST_PAD)
    if is_sample:
        pos1 = None
    else:
        pos1 = (j * tm + 1 + lax.broadcasted_iota(jnp.int32, (tm, 1), 0)).astype(jnp.float32)
    s = ext
    a_parts = []
    for g, w in enumerate(POOL_WINDOWS):
        s = s + pltpu.roll(s, w // 2, axis=0)
        lo = g * POOL_GROUP_DIM
        win = take_new(s[:, :POOL_GROUP_DIM])
        if is_sample:
            mean = win * (1.0 / w)
        else:
            mean = win / jnp.minimum(jnp.float32(w), pos1)
        d = mean - p[:, lo:lo + POOL_GROUP_DIM]
        a_parts.append(_dot(d.astype(jnp.bfloat16), w_pool_ref[g]))
        if g + 1 < len(POOL_WINDOWS):
            s = s[:, POOL_GROUP_DIM:]
    a_out = jnp.concatenate(a_parts, axis=1) * pscale_ref[...]

    cz = gate_c * z
    hist_c = carried_history(hc_ref, CONV_HIST) if is_sample else hc_scr[...]
    ext_c, take_new_c = with_history(hist_c, cz, CONV_HIST_PAD)
    y = w_conv_ref[CONV_K - 1:CONV_K, :] * ext_c
    for k in range(CONV_K - 1):
        y = y + w_conv_ref[k:k + 1, :] * pltpu.roll(ext_c, CONV_K - 1 - k, axis=0)
    c_out = gate_b * take_new_c(y)

    o = _dot(c_out.astype(jnp.bfloat16), w_out_ref[POOL_WIDTH + GMLP_WIDTH:, :])
    o = o + _dot(a_out.astype(jnp.bfloat16), w_out_ref[:POOL_WIDTH, :])
    o = o + _dot(b_out.astype(jnp.bfloat16), w_out_ref[POOL_WIDTH:POOL_WIDTH + GMLP_WIDTH, :])
    r_ref[...] = ALPHA * x + o

    if is_sample:
        p_out_ref[...] = p
        cz_out_ref[...] = cz
    else:
        hp_new = p[tm - POOL_HIST_PAD:, :]
        hc_new = cz[tm - CONV_HIST_PAD:, :]
        hp_scr[...] = hp_new
        hc_scr[...] = hc_new
        p_out_ref[...] = hp_new
        cz_out_ref[...] = hc_new


def _layer_spec(layer, shape):
    nd = len(shape)
    return pl.BlockSpec((None,) + shape, lambda *_: (layer,) + (0,) * nd, pipeline_mode=pl.Buffered(1))


def _mixer_weight_specs(layer, wide_slab):
    spec = functools.partial(_layer_spec, layer)
    return [
        _layer_spec(wide_slab, (D_MODEL, IN_WIDTH)),
        spec((len(POOL_WINDOWS), POOL_GROUP_DIM, POOL_GROUP_DIM)),
        spec((1, POOL_WIDTH)),
        spec((1, GMLP_WIDTH)),
        spec((1, GMLP_WIDTH)),
        spec((GMLP_HEADS, CHUNK, CHUNK)),
        spec((CHUNK, GMLP_WIDTH)),
        spec((CONV_K, CONV_WIDTH)),
        _layer_spec(wide_slab, (D_MODEL, D_MODEL)),
    ]


def _mixer_prompt(layer, wide_slab, x, weights, batch, seq):
    tm = MIX_TILE_ROWS
    nj = seq // tm
    row_spec = lambda width: pl.BlockSpec((tm, width), lambda b, j: (b * nj + j, 0))
    return pl.pallas_call(
        functools.partial(_mixer_kernel, False, tm),
        grid=(batch, nj),
        in_specs=[row_spec(D_MODEL)] + _mixer_weight_specs(layer, wide_slab),
        out_specs=[
            row_spec(D_MODEL),
            pl.BlockSpec((None, POOL_HIST_PAD, POOL_WIDTH), lambda b, j: (b, 0, 0)),
            pl.BlockSpec((None, CONV_HIST_PAD, CONV_WIDTH), lambda b, j: (b, 0, 0)),
        ],
        out_shape=[
            jax.ShapeDtypeStruct((batch * seq, D_MODEL), jnp.float32),
            jax.ShapeDtypeStruct((batch, POOL_HIST_PAD, POOL_WIDTH), jnp.float32),
            jax.ShapeDtypeStruct((batch, CONV_HIST_PAD, CONV_WIDTH), jnp.float32),
        ],
        scratch_shapes=[
            pltpu.VMEM((POOL_HIST_PAD, POOL_WIDTH), jnp.float32),
            pltpu.VMEM((CONV_HIST_PAD, CONV_WIDTH), jnp.float32),
        ],
        compiler_params=pltpu.CompilerParams(
            dimension_semantics=("arbitrary", "arbitrary"), vmem_limit_bytes=V7X_VMEM_LIMIT_BYTES),
        name="mixer_prompt",
    )(x, *weights)


def _mixer_sample(layer, wide_slab, x, state_pool, state_conv, weights):
    m = x.shape[0]
    tm = SAMPLE_TILE_ROWS
    nseq = state_pool.shape[1] * tm // m
    rows = lambda width: pl.BlockSpec((tm, width), lambda i: (i, 0))
    seqs = lambda pad, width: pl.BlockSpec((None, nseq, pad, width), lambda i: (layer, i, 0, 0))
    return pl.pallas_call(
        functools.partial(_mixer_kernel, True, tm),
        grid=(m // tm,),
        in_specs=[rows(D_MODEL), seqs(POOL_HIST_PAD, POOL_WIDTH),
                  seqs(CONV_HIST_PAD, CONV_WIDTH)] + _mixer_weight_specs(layer, wide_slab),
        out_specs=[rows(D_MODEL), rows(POOL_WIDTH), rows(CONV_WIDTH), rows(GMLP_WIDTH)],
        out_shape=[
            jax.ShapeDtypeStruct((m, D_MODEL), jnp.float32),
            jax.ShapeDtypeStruct((m, POOL_WIDTH), jnp.float32),
            jax.ShapeDtypeStruct((m, CONV_WIDTH), jnp.float32),
            jax.ShapeDtypeStruct((m, GMLP_WIDTH), jnp.float32),
        ],
        compiler_params=pltpu.CompilerParams(
            dimension_semantics=("arbitrary",), vmem_limit_bytes=V7X_VMEM_LIMIT_BYTES),
        name="mixer_sample",
    )(x, state_pool, state_conv, *weights)


def _ffn_kernel(n_p, n_s, n_next, rp_hbm, rp_ref, rs_ref, w1_ref, w2_ref, g1_ref, b1_ref, g2_ref, b2_ref, *rest):
    next_f32 = rest[:n_next]
    yp_ref, ys_ref = rest[n_next:n_next + 2]
    next_bf16 = rest[n_next + 2:2 * n_next + 2]
    acc_even, acc_odd, hb_even, hb_odd, sem = rest[2 * n_next + 2:]
    i, f = pl.program_id(0), pl.program_id(1)
    n = n_p + n_s
    tm = acc_even.shape[0]
    piece = rp_ref.shape[0]
    rows = pl.ds(pl.multiple_of(f * piece, piece), piece)

    def seed(acc_ref, hb_ref, dst_rows, r):
        h = _layernorm(r, g1_ref[...], b1_ref[...])
        hb_ref[dst_rows, :] = h.astype(jnp.bfloat16)
        acc_ref[dst_rows, :] = ALPHA * h
        return h

    def finished_piece(acc_ref):
        return _layernorm(acc_ref[rows, :], g2_ref[...], b2_ref[...])

    def zero_after(*values):
        tiles = []
        for v in values:
            bits = lax.bitcast_convert_type(v, jnp.uint32)
            tiles += [bits[r:r + 8, c:c + 128] for r in range(0, v.shape[0], 8) for c in range(0, v.shape[1], 128)]
        word = functools.reduce(lambda p, q: p | q, tiles)
        return lax.bitcast_convert_type((word >> 16) >> 16, jnp.float32)

    def step(acc_cur, hb_cur, acc_oth, hb_oth):
        y = finished_piece(acc_oth)
        yp_ref[...] = y
        h_next = seed(acc_oth, hb_oth, rows, jnp.where(i + 1 < n_p, rp_ref[...], rs_ref[...]))
        for src, dst in zip(next_f32, next_bf16):
            dst[...] = src[...].astype(jnp.bfloat16)

        w1, w2 = w1_ref[...], w2_ref[...]
        if w1.dtype != jnp.bfloat16:
            w1, w2 = w1.astype(jnp.bfloat16), w2.astype(jnp.bfloat16)
        tf = w1.shape[1]
        floor = jnp.concatenate([zero_after(y, h_next)] * (tf // 128), axis=1)
        floor = jnp.broadcast_to(floor[None], (tm // 8, 8, tf)).reshape(tm, tf)
        a = jnp.square(jnp.maximum(_dot(hb_cur[...], w1), floor))
        acc_cur[...] += _dot(a.astype(jnp.bfloat16), w2)

    @pl.when((i == 0) & (f == 0))
    def _():
        first = pltpu.make_async_copy(rp_hbm.at[pl.ds(0, tm), :], acc_even, sem)
        first.start()
        first.wait()
        seed(acc_even, hb_even, slice(None), acc_even[...])
        acc_odd[...] = jnp.zeros_like(acc_odd)

    @pl.when((i < n) & (i % 2 == 0))
    def _():
        step(acc_even, hb_even, acc_odd, hb_odd)

    @pl.when((i < n) & (i % 2 == 1))
    def _():
        step(acc_odd, hb_odd, acc_even, hb_even)

    @pl.when(i == n)
    def _():
        ys_ref[...] = finished_piece(acc_odd if n % 2 == 0 else acc_even)


def _ffn(layer, r_p, r_s, w1, w2, w_slab, g1, b1, g2, b2, next_weights):
    tm = FFN_TILE_ROWS
    wide = w1.dtype == jnp.bfloat16
    tf = FFN_TILE_COLS_BF16 if wide else FFN_TILE_COLS_F32
    n_p, n_s = r_p.shape[0] // tm, r_s.shape[0] // tm
    nf = D_FF // tf
    piece = tm // nf
    last = n_p + n_s - 1

    def piece_spec(first_tile, n_tiles, lag, single_buffer=False):
        def index(i, f):
            return (jnp.clip((i + lag - first_tile) * nf + f, 0, n_tiles * nf - 1), 0)
        mode = dict(pipeline_mode=pl.Buffered(1)) if single_buffer else {}
        return pl.BlockSpec((piece, D_MODEL), index, **mode)

    sample_single = wide and bool(next_weights)

    def slab_index(i, f):
        return jnp.minimum(i * nf + f, convert_steps - 1)

    assert n_s == 1
    convert_steps = 1 << (((last + 1) * nf).bit_length() - 1)
    next_in, next_out, next_shapes = [], [], []
    for w in next_weights:
        slab_rows = w.shape[1] // convert_steps
        assert slab_rows % BF16_SUBLANE_TILE == 0
        next_in.append(pl.BlockSpec((None, slab_rows, w.shape[2]), lambda i, f: (layer + 1, slab_index(i, f), 0)))
        next_out.append(pl.BlockSpec((None, slab_rows, w.shape[2]), lambda i, f: (0, slab_index(i, f), 0)))
        next_shapes.append(jax.ShapeDtypeStruct((1,) + w.shape[1:], jnp.bfloat16))

    vec = pl.BlockSpec((None, 1, D_MODEL), lambda i, f: (layer, 0, 0))
    chunk = lambda i, f: jnp.where(i > last, nf - 1, f)
    return pl.pallas_call(
        functools.partial(_ffn_kernel, n_p, n_s, len(next_weights)),
        grid=(n_p + n_s + 1, nf),
        in_specs=[
            pl.BlockSpec(memory_space=pl.ANY),
            piece_spec(0, n_p, 1),
            piece_spec(n_p, n_s, 1, sample_single),
            pl.BlockSpec((None, D_MODEL, tf), lambda i, f: (w_slab, 0, chunk(i, f))),
            pl.BlockSpec((None, tf, D_MODEL), lambda i, f: (w_slab, chunk(i, f), 0)),
            vec, vec, vec, vec,
        ] + next_in,
        out_specs=[piece_spec(0, n_p, -1), piece_spec(n_p, n_s, -1, sample_single)] + next_out,
        out_shape=[jax.ShapeDtypeStruct(r_p.shape, jnp.float32),
                   jax.ShapeDtypeStruct(r_s.shape, jnp.float32)] + next_shapes,
        scratch_shapes=[pltpu.VMEM((tm, D_MODEL), jnp.float32), pltpu.VMEM((tm, D_MODEL), jnp.float32),
                        pltpu.VMEM((tm, D_MODEL), jnp.bfloat16), pltpu.VMEM((tm, D_MODEL), jnp.bfloat16),
                        pltpu.SemaphoreType.DMA(())],
        compiler_params=pltpu.CompilerParams(
            dimension_semantics=("arbitrary", "arbitrary"), vmem_limit_bytes=V7X_VMEM_LIMIT_BYTES),
        name="ffn",
    )(r_p, r_p, r_s, w1, w2, g1, b1, g2, b2, *next_weights)


def kernel(x_prompt, x_sample, state_pool, state_conv, w_in, w_pool, pool_scale, ln_v_g, ln_v_b, w_s, b_s, w_conv, w_out, ln1_g, ln1_b, w_ff1, w_ff2, ln2_g, ln2_b):
    bf16 = jnp.bfloat16
    batch, seq, _ = x_prompt.shape
    nseq, dec_len, _ = x_sample.shape
    assert seq % MIX_TILE_ROWS == 0 and MIX_TILE_ROWS % CHUNK == 0
    assert CHUNK % dec_len == 0 and PAST_LEN % CHUNK == 0
    assert (nseq * dec_len) % SAMPLE_TILE_ROWS == 0 and SAMPLE_TILE_ROWS % CHUNK == 0
    assert (batch * seq) % FFN_TILE_ROWS == 0 and (nseq * dec_len) % FFN_TILE_ROWS == 0

    w_pool_b = w_pool.astype(bf16)
    row = lambda a: a.reshape(DEPTH, 1, -1)
    pool_scale, ln_v_g, ln_v_b = row(pool_scale), row(ln_v_g), row(ln_v_b)
    ln1_g, ln1_b, ln2_g, ln2_b = row(ln1_g), row(ln1_b), row(ln2_g), row(ln2_b)

    exact = lax.Precision.HIGHEST
    head_cols = (jnp.arange(GMLP_WIDTH)[None, :] // GMLP_HEAD_DIM == jnp.arange(GMLP_HEADS)[:, None]).astype(jnp.float32)
    pos_rows = (jnp.arange(CHUNK)[:, None] % dec_len == jnp.arange(dec_len)[None, :]).astype(jnp.float32)
    bs_prompt = jnp.einsum('lhi,hc->lic', b_s, head_cols, precision=exact)
    bs_sample = jnp.einsum('ik,lhk,hc->lic', pos_rows, b_s[:, :, :dec_len], head_cols, precision=exact)
    ws_sample = jnp.einsum('ik,lhkm,jm->lhij', pos_rows, w_s[:, :, :dec_len, :dec_len], pos_rows, precision=exact)

    xp = x_prompt.reshape(batch * seq, D_MODEL)
    xs = x_sample.reshape(nseq * dec_len, D_MODEL)
    pool_p, conv_p, p_new, cz_new, chunk_v = [], [], [], [], []
    w_in_b, w_out_b = w_in[:1].astype(bf16), w_out[:1].astype(bf16)
    w1, w2 = w_ff1, w_ff2
    for l in range(DEPTH):
        shared = (w_in_b, w_pool_b, pool_scale, ln_v_g, ln_v_b)
        tail = (w_conv, w_out_b)
        rp, p16, cz8 = _mixer_prompt(l, 0, xp, shared + (w_s, bs_prompt) + tail, batch, seq)
        rs, p_s, cz_s, v_s = _mixer_sample(l, 0, xs, state_pool, state_conv,
                                           shared + (ws_sample, bs_sample) + tail)
        next_weights = (w_in, w_out, w_ff1, w_ff2) if l + 1 < DEPTH else ()
        xp, xs, *rounded = _ffn(l, rp, rs, w1, w2, 0, ln1_g, ln1_b, ln2_g, ln2_b, next_weights)
        if rounded:
            w_in_b, w_out_b, w1, w2 = rounded
        pool_p.append(p16[:, POOL_HIST_PAD - POOL_HIST:])
        conv_p.append(cz8[:, CONV_HIST_PAD - CONV_HIST:])
        p_new.append(p_s.reshape(nseq, dec_len, POOL_WIDTH))
        cz_new.append(cz_s.reshape(nseq, dec_len, CONV_WIDTH))
        chunk_v.append(v_s.reshape(nseq, dec_len, GMLP_WIDTH))

    pool_s = jnp.concatenate([state_pool, jnp.stack(p_new)], axis=2)[:, :, -POOL_HIST:]
    conv_s = jnp.concatenate([state_conv, jnp.stack(cz_new)], axis=2)[:, :, -CONV_HIST:]
    return (xp.reshape(batch, seq, D_MODEL), xs.reshape(nseq, dec_len, D_MODEL),
            jnp.stack(pool_p), jnp.stack(conv_p), pool_s, conv_s, jnp.stack(chunk_v))
```

```python
import functools

import jax
import jax.numpy as jnp
from jax import lax
from jax.experimental import pallas as pl
from jax.experimental.pallas import tpu as pltpu

D_MODEL = 2048
DEPTH = 4
POOL_WIDTH = D_MODEL // 4
POOL_WINDOWS = (2, 4, 8, 16)
POOL_GROUP_DIM = POOL_WIDTH // len(POOL_WINDOWS)
POOL_HIST = max(POOL_WINDOWS) - 1
GMLP_WIDTH = D_MODEL // 2
GMLP_HEADS = 8
GMLP_HEAD_DIM = GMLP_WIDTH // GMLP_HEADS
CHUNK = 128
CONV_WIDTH = D_MODEL // 4
CONV_K = 3
CONV_HIST = CONV_K - 1
D_FF = 4 * D_MODEL
PAST_LEN = 16384
ALPHA = (2 * DEPTH) ** 0.25
LN_EPS = 1e-5

OFF_P = 0
OFF_U = OFF_P + POOL_WIDTH
OFF_V = OFF_U + GMLP_WIDTH
OFF_GB = OFF_V + GMLP_WIDTH
OFF_GC = OFF_GB + CONV_WIDTH
OFF_Z = OFF_GC + CONV_WIDTH
IN_WIDTH = OFF_Z + CONV_WIDTH

POOL_HIST_PAD = 16
CONV_HIST_PAD = 8

V7X_VMEM_LIMIT_BYTES = 60 * 1024 * 1024

MIX_TILE_ROWS = 512
SAMPLE_TILE_ROWS = 256
FFN_TILE_ROWS = 1024
FFN_TILE_COLS_F32 = 512
FFN_TILE_COLS_BF16 = 1024
BF16_SUBLANE_TILE = 16


def _dot(a, b):
    return jnp.dot(a, b, preferred_element_type=jnp.float32)


def _layernorm(x, g, b):
    mu = jnp.mean(x, axis=-1, keepdims=True)
    xc = x - mu
    var = jnp.mean(xc * xc, axis=-1, keepdims=True)
    return xc * lax.rsqrt(var + LN_EPS) * g + b


def _mixer_kernel(is_sample, tm, *refs):
    if is_sample:
        (x_ref, hp_ref, hc_ref, w_in_ref, w_pool_ref, pscale_ref, lnv_g_ref, lnv_b_ref, ws_ref, bs_ref,
         w_conv_ref, w_out_ref,
         r_ref, p_out_ref, cz_out_ref, v_out_ref) = refs
        nseq = hp_ref.shape[0]
        seq_len = tm // nseq

        def carried_history(ref, state_rows):
            _, pad_rows, width = ref.shape
            flat = ref[...].reshape(nseq * pad_rows, width)
            return pltpu.roll(flat, pad_rows - state_rows, axis=0).reshape(nseq, pad_rows, width)
    else:
        n_next = (len(refs) - 15) // 2
        (x_ref, w_in_ref, w_pool_ref, pscale_ref, lnv_g_ref, lnv_b_ref, ws_ref, bs_ref,
         w_conv_ref, w_out_ref) = refs[:10]
        next_f32 = refs[10:10 + n_next]
        r_ref, p_out_ref, cz_out_ref = refs[10 + n_next:13 + n_next]
        next_bf16 = refs[13 + n_next:13 + 2 * n_next]
        hp_scr, hc_scr = refs[13 + 2 * n_next:]
        for src, dst in zip(next_f32, next_bf16):
            dst[...] = src[...].astype(jnp.bfloat16)
        j = pl.program_id(1)

        @pl.when(j == 0)
        def _():
            hp_scr[...] = jnp.zeros_like(hp_scr)
            hc_scr[...] = jnp.zeros_like(hc_scr)

    x = x_ref[...]
    xb = x.astype(jnp.bfloat16)

    def proj(off, width):
        return _dot(xb, w_in_ref[:, off:off + width])

    def with_history(hist, new, hist_rows):
        width = new.shape[-1]
        if is_sample:
            ext = jnp.concatenate([hist, new.reshape(nseq, seq_len, width)], axis=1)
            ext = ext.reshape(nseq * (hist_rows + seq_len), width)

            def take_new(s):
                w = s.shape[-1]
                return s.reshape(nseq, hist_rows + seq_len, w)[:, hist_rows:, :].reshape(tm, w)
        else:
            ext = jnp.concatenate([hist, new], axis=0)

            def take_new(s):
                return s[hist_rows:, :]
        return ext, take_new

    v_pre = proj(OFF_V, GMLP_WIDTH)
    p = proj(OFF_P, POOL_WIDTH)
    gate_c = proj(OFF_GC, CONV_WIDTH)
    z = proj(OFF_Z, CONV_WIDTH)
    gate_b = proj(OFF_GB, CONV_WIDTH)
    u_pre = proj(OFF_U, GMLP_WIDTH)

    v = _layernorm(jax.nn.gelu(v_pre), lnv_g_ref[...], lnv_b_ref[...])
    if is_sample:
        v_out_ref[...] = v
    vb = v.astype(jnp.bfloat16)
    row = lax.broadcasted_iota(jnp.int32, (CHUNK, CHUNK), 0)
    col = lax.broadcasted_iota(jnp.int32, (CHUNK, CHUNK), 1)
    if is_sample:
        mask = (row // seq_len == col // seq_len) & (col <= row)
    else:
        mask = col <= row
    nchunk = tm // CHUNK
    last = lax.bitcast_convert_type(u_pre[tm - CHUNK:, GMLP_WIDTH - CHUNK:], jnp.uint32)
    zero_bits = (last >> 16) >> 16
    mixed_cols = []
    for hd in range(GMLP_HEADS):
        wm = jnp.where(mask, ws_ref[hd], 0.0)
        wm = lax.bitcast_convert_type(lax.bitcast_convert_type(wm, jnp.uint32) | zero_bits, jnp.float32)
        wm = wm.astype(jnp.bfloat16)
        lo = hd * GMLP_HEAD_DIM
        rhs = jnp.concatenate(
            [vb[c * CHUNK:(c + 1) * CHUNK, lo:lo + GMLP_HEAD_DIM] for c in range(nchunk)], axis=1)
        out = _dot(wm, rhs)
        mixed_cols.append(jnp.concatenate(
            [out[:, c * GMLP_HEAD_DIM:(c + 1) * GMLP_HEAD_DIM] for c in range(nchunk)], axis=0))
    mixed = jnp.concatenate(mixed_cols, axis=1)
    bias = jnp.concatenate([bs_ref[...]] * nchunk, axis=0)
    b_out = jax.nn.gelu(u_pre) * (mixed + bias)

    hist_p = carried_history(hp_ref, POOL_HIST) if is_sample else hp_scr[...]
    ext, take_new = with_history(hist_p, p, POOL_HIST_PAD)
    if is_sample:
        pos1 = None
    else:
        pos1 = (j * tm + 1 + lax.broadcasted_iota(jnp.int32, (tm, 1), 0)).astype(jnp.float32)
    s = ext
    a_parts = []
    for g, w in enumerate(POOL_WINDOWS):
        s = s + pltpu.roll(s, w // 2, axis=0)
        lo = g * POOL_GROUP_DIM
        win = take_new(s[:, :POOL_GROUP_DIM])
        if is_sample:
            mean = win * (1.0 / w)
        else:
            mean = win / jnp.minimum(jnp.float32(w), pos1)
        d = mean - p[:, lo:lo + POOL_GROUP_DIM]
        a_parts.append(_dot(d.astype(jnp.bfloat16), w_pool_ref[g]))
        if g + 1 < len(POOL_WINDOWS):
            s = s[:, POOL_GROUP_DIM:]
    a_out = jnp.concatenate(a_parts, axis=1) * pscale_ref[...]

    cz = gate_c * z
    hist_c = carried_history(hc_ref, CONV_HIST) if is_sample else hc_scr[...]
    ext_c, take_new_c = with_history(hist_c, cz, CONV_HIST_PAD)
    y = w_conv_ref[CONV_K - 1:CONV_K, :] * ext_c
    for k in range(CONV_K - 1):
        y = y + w_conv_ref[k:k + 1, :] * pltpu.roll(ext_c, CONV_K - 1 - k, axis=0)
    c_out = gate_b * take_new_c(y)

    o = _dot(c_out.astype(jnp.bfloat16), w_out_ref[POOL_WIDTH + GMLP_WIDTH:, :])
    o = o + _dot(a_out.astype(jnp.bfloat16), w_out_ref[:POOL_WIDTH, :])
    o = o + _dot(b_out.astype(jnp.bfloat16), w_out_ref[POOL_WIDTH:POOL_WIDTH + GMLP_WIDTH, :])
    r_ref[...] = ALPHA * x + o

    if is_sample:
        p_out_ref[...] = p
        cz_out_ref[...] = cz
    else:
        hp_new = p[tm - POOL_HIST_PAD:, :]
        hc_new = cz[tm - CONV_HIST_PAD:, :]
        hp_scr[...] = hp_new
        hc_scr[...] = hc_new
        p_out_ref[...] = hp_new
        cz_out_ref[...] = hc_new


def _layer_spec(layer, shape):
    nd = len(shape)
    return pl.BlockSpec((None,) + shape, lambda *_: (layer,) + (0,) * nd, pipeline_mode=pl.Buffered(1))


def _mixer_weight_specs(layer, wide_slab):
    spec = functools.partial(_layer_spec, layer)
    return [
        _layer_spec(wide_slab, (D_MODEL, IN_WIDTH)),
        spec((len(POOL_WINDOWS), POOL_GROUP_DIM, POOL_GROUP_DIM)),
        spec((1, POOL_WIDTH)),
        spec((1, GMLP_WIDTH)),
        spec((1, GMLP_WIDTH)),
        spec((GMLP_HEADS, CHUNK, CHUNK)),
        spec((CHUNK, GMLP_WIDTH)),
        spec((CONV_K, CONV_WIDTH)),
        _layer_spec(wide_slab, (D_MODEL, D_MODEL)),
    ]


def _mixer_prompt(layer, wide_slab, x, weights, batch, seq, next_weights):
    tm = MIX_TILE_ROWS
    nj = seq // tm
    row_spec = lambda width: pl.BlockSpec((tm, width), lambda b, j: (b * nj + j, 0))
    next_in, next_out, next_shapes = [], [], []
    for w in next_weights:
        slab_rows = w.shape[1] // (batch * nj)
        assert slab_rows % BF16_SUBLANE_TILE == 0
        next_in.append(pl.BlockSpec((None, slab_rows, w.shape[2]), lambda b, j: (layer + 1, b * nj + j, 0)))
        next_out.append(pl.BlockSpec((None, slab_rows, w.shape[2]), lambda b, j: (0, b * nj + j, 0)))
        next_shapes.append(jax.ShapeDtypeStruct((1,) + w.shape[1:], jnp.bfloat16))
    return pl.pallas_call(
        functools.partial(_mixer_kernel, False, tm),
        grid=(batch, nj),
        in_specs=[row_spec(D_MODEL)] + _mixer_weight_specs(layer, wide_slab) + next_in,
        out_specs=[
            row_spec(D_MODEL),
            pl.BlockSpec((None, POOL_HIST_PAD, POOL_WIDTH), lambda b, j: (b, 0, 0)),
            pl.BlockSpec((None, CONV_HIST_PAD, CONV_WIDTH), lambda b, j: (b, 0, 0)),
        ] + next_out,
        out_shape=[
            jax.ShapeDtypeStruct((batch * seq, D_MODEL), jnp.float32),
            jax.ShapeDtypeStruct((batch, POOL_HIST_PAD, POOL_WIDTH), jnp.float32),
            jax.ShapeDtypeStruct((batch, CONV_HIST_PAD, CONV_WIDTH), jnp.float32),
        ] + next_shapes,
        scratch_shapes=[
            pltpu.VMEM((POOL_HIST_PAD, POOL_WIDTH), jnp.float32),
            pltpu.VMEM((CONV_HIST_PAD, CONV_WIDTH), jnp.float32),
        ],
        compiler_params=pltpu.CompilerParams(
            dimension_semantics=("arbitrary", "arbitrary"), vmem_limit_bytes=V7X_VMEM_LIMIT_BYTES),
        name="mixer_prompt",
    )(x, *weights, *next_weights)


def _mixer_sample(layer, wide_slab, x, state_pool, state_conv, weights):
    m = x.shape[0]
    tm = SAMPLE_TILE_ROWS
    nseq = state_pool.shape[1] * tm // m
    rows = lambda width: pl.BlockSpec((tm, width), lambda i: (i, 0))
    seqs = lambda pad, width: pl.BlockSpec((None, nseq, pad, width), lambda i: (layer, i, 0, 0))
    return pl.pallas_call(
        functools.partial(_mixer_kernel, True, tm),
        grid=(m // tm,),
        in_specs=[rows(D_MODEL), seqs(POOL_HIST_PAD, POOL_WIDTH),
                  seqs(CONV_HIST_PAD, CONV_WIDTH)] + _mixer_weight_specs(layer, wide_slab),
        out_specs=[rows(D_MODEL), rows(POOL_WIDTH), rows(CONV_WIDTH), rows(GMLP_WIDTH)],
        out_shape=[
            jax.ShapeDtypeStruct((m, D_MODEL), jnp.float32),
            jax.ShapeDtypeStruct((m, POOL_WIDTH), jnp.float32),
            jax.ShapeDtypeStruct((m, CONV_WIDTH), jnp.float32),
            jax.ShapeDtypeStruct((m, GMLP_WIDTH), jnp.float32),
        ],
        compiler_params=pltpu.CompilerParams(
            dimension_semantics=("arbitrary",), vmem_limit_bytes=V7X_VMEM_LIMIT_BYTES),
        name="mixer_sample",
    )(x, state_pool, state_conv, *weights)


def _ffn_kernel(n_p, n_s, n_next, rp_hbm, rp_ref, rs_ref, w1_ref, w2_ref, g1_ref, b1_ref, g2_ref, b2_ref, *rest):
    next_f32 = rest[:n_next]
    yp_ref, ys_ref = rest[n_next:n_next + 2]
    next_bf16 = rest[n_next + 2:2 * n_next + 2]
    acc_even, acc_odd, hb_even, hb_odd, sem = rest[2 * n_next + 2:]
    i, f = pl.program_id(0), pl.program_id(1)
    n = n_p + n_s
    tm = acc_even.shape[0]
    piece = rp_ref.shape[0]
    rows = pl.ds(pl.multiple_of(f * piece, piece), piece)

    def seed(acc_ref, hb_ref, dst_rows, r):
        h = _layernorm(r, g1_ref[...], b1_ref[...])
        hb_ref[dst_rows, :] = h.astype(jnp.bfloat16)
        acc_ref[dst_rows, :] = ALPHA * h
        return h

    def finished_piece(acc_ref):
        return _layernorm(acc_ref[rows, :], g2_ref[...], b2_ref[...])

    def zero_after(*values):
        tiles = []
        for v in values:
            bits = lax.bitcast_convert_type(v, jnp.uint32)
            tiles += [bits[r:r + 8, c:c + 128] for r in range(0, v.shape[0], 8) for c in range(0, v.shape[1], 128)]
        word = functools.reduce(lambda p, q: p | q, tiles)
        return lax.bitcast_convert_type((word >> 16) >> 16, jnp.float32)

    def step(acc_cur, hb_cur, acc_oth, hb_oth):
        y = finished_piece(acc_oth)
        yp_ref[...] = y
        h_next = seed(acc_oth, hb_oth, rows, jnp.where(i + 1 < n_p, rp_ref[...], rs_ref[...]))
        for src, dst in zip(next_f32, next_bf16):
            dst[...] = src[...].astype(jnp.bfloat16)

        w1, w2 = w1_ref[...], w2_ref[...]
        if w1.dtype != jnp.bfloat16:
            w1, w2 = w1.astype(jnp.bfloat16), w2.astype(jnp.bfloat16)
        tf = w1.shape[1]
        floor = jnp.concatenate([zero_after(y, h_next)] * (tf // 128), axis=1)
        floor = jnp.broadcast_to(floor[None], (tm // 8, 8, tf)).reshape(tm, tf)
        a = jnp.square(jnp.maximum(_dot(hb_cur[...], w1), floor))
        acc_cur[...] += _dot(a.astype(jnp.bfloat16), w2)

    @pl.when((i == 0) & (f == 0))
    def _():
        first = pltpu.make_async_copy(rp_hbm.at[pl.ds(0, tm), :], acc_even, sem)
        first.start()
        first.wait()
        seed(acc_even, hb_even, slice(None), acc_even[...])
        acc_odd[...] = jnp.zeros_like(acc_odd)

    @pl.when((i < n) & (i % 2 == 0))
    def _():
        step(acc_even, hb_even, acc_odd, hb_odd)

    @pl.when((i < n) & (i % 2 == 1))
    def _():
        step(acc_odd, hb_odd, acc_even, hb_even)

    @pl.when(i == n)
    def _():
        ys_ref[...] = finished_piece(acc_odd if n % 2 == 0 else acc_even)


def _ffn(layer, r_p, r_s, w1, w2, w_slab, g1, b1, g2, b2, next_weights):
    tm = FFN_TILE_ROWS
    tf = FFN_TILE_COLS_BF16 if w1.dtype == jnp.bfloat16 else FFN_TILE_COLS_F32
    n_p, n_s = r_p.shape[0] // tm, r_s.shape[0] // tm
    nf = D_FF // tf
    piece = tm // nf
    last = n_p + n_s - 1

    def piece_spec(first_tile, n_tiles, lag):
        def index(i, f):
            return (jnp.clip((i + lag - first_tile) * nf + f, 0, n_tiles * nf - 1), 0)
        return pl.BlockSpec((piece, D_MODEL), index)

    def slab_index(i, f):
        return jnp.minimum(i * nf + f, convert_steps - 1)

    assert n_s == 1
    convert_steps = 1 << (((last + 1) * nf).bit_length() - 1)
    next_in, next_out, next_shapes = [], [], []
    for w in next_weights:
        slab_rows = w.shape[1] // convert_steps
        assert slab_rows % BF16_SUBLANE_TILE == 0
        next_in.append(pl.BlockSpec((None, slab_rows, w.shape[2]), lambda i, f: (layer + 1, slab_index(i, f), 0)))
        next_out.append(pl.BlockSpec((None, slab_rows, w.shape[2]), lambda i, f: (0, slab_index(i, f), 0)))
        next_shapes.append(jax.ShapeDtypeStruct((1,) + w.shape[1:], jnp.bfloat16))

    vec = pl.BlockSpec((None, 1, D_MODEL), lambda i, f: (layer, 0, 0))
    chunk = lambda i, f: jnp.where(i > last, nf - 1, f)
    return pl.pallas_call(
        functools.partial(_ffn_kernel, n_p, n_s, len(next_weights)),
        grid=(n_p + n_s + 1, nf),
        in_specs=[
            pl.BlockSpec(memory_space=pl.ANY),
            piece_spec(0, n_p, 1),
            piece_spec(n_p, n_s, 1),
            pl.BlockSpec((None, D_MODEL, tf), lambda i, f: (w_slab, 0, chunk(i, f))),
            pl.BlockSpec((None, tf, D_MODEL), lambda i, f: (w_slab, chunk(i, f), 0)),
            vec, vec, vec, vec,
        ] + next_in,
        out_specs=[piece_spec(0, n_p, -1), piece_spec(n_p, n_s, -1)] + next_out,
        out_shape=[jax.ShapeDtypeStruct(r_p.shape, jnp.float32),
                   jax.ShapeDtypeStruct(r_s.shape, jnp.float32)] + next_shapes,
        scratch_shapes=[pltpu.VMEM((tm, D_MODEL), jnp.float32), pltpu.VMEM((tm, D_MODEL), jnp.float32),
                        pltpu.VMEM((tm, D_MODEL), jnp.bfloat16), pltpu.VMEM((tm, D_MODEL), jnp.bfloat16),
                        pltpu.SemaphoreType.DMA(())],
        compiler_params=pltpu.CompilerParams(
            dimension_semantics=("arbitrary", "arbitrary"), vmem_limit_bytes=V7X_VMEM_LIMIT_BYTES),
        name="ffn",
    )(r_p, r_p, r_s, w1, w2, g1, b1, g2, b2, *next_weights)


def kernel(x_prompt, x_sample, state_pool, state_conv, w_in, w_pool, pool_scale, ln_v_g, ln_v_b, w_s, b_s, w_conv, w_out, ln1_g, ln1_b, w_ff1, w_ff2, ln2_g, ln2_b):
    bf16 = jnp.bfloat16
    batch, seq, _ = x_prompt.shape
    nseq, dec_len, _ = x_sample.shape
    assert seq % MIX_TILE_ROWS == 0 and MIX_TILE_ROWS % CHUNK == 0
    assert CHUNK % dec_len == 0 and PAST_LEN % CHUNK == 0
    assert (nseq * dec_len) % SAMPLE_TILE_ROWS == 0 and SAMPLE_TILE_ROWS % CHUNK == 0
    assert (batch * seq) % FFN_TILE_ROWS == 0 and (nseq * dec_len) % FFN_TILE_ROWS == 0

    w_pool_b = w_pool.astype(bf16)
    row = lambda a: a.reshape(DEPTH, 1, -1)
    pool_scale, ln_v_g, ln_v_b = row(pool_scale), row(ln_v_g), row(ln_v_b)
    ln1_g, ln1_b, ln2_g, ln2_b = row(ln1_g), row(ln1_b), row(ln2_g), row(ln2_b)

    exact = lax.Precision.HIGHEST
    head_cols = (jnp.arange(GMLP_WIDTH)[None, :] // GMLP_HEAD_DIM == jnp.arange(GMLP_HEADS)[:, None]).astype(jnp.float32)
    pos_rows = (jnp.arange(CHUNK)[:, None] % dec_len == jnp.arange(dec_len)[None, :]).astype(jnp.float32)
    bs_prompt = jnp.einsum('lhi,hc->lic', b_s, head_cols, precision=exact)
    bs_sample = jnp.einsum('ik,lhk,hc->lic', pos_rows, b_s[:, :, :dec_len], head_cols, precision=exact)
    ws_sample = jnp.einsum('ik,lhkm,jm->lhij', pos_rows, w_s[:, :, :dec_len, :dec_len], pos_rows, precision=exact)

    xp = x_prompt.reshape(batch * seq, D_MODEL)
    xs = x_sample.reshape(nseq * dec_len, D_MODEL)
    pool_p, conv_p, p_new, cz_new, chunk_v = [], [], [], [], []
    w_in_b, w_out_b = w_in[:1].astype(bf16), w_out[:1].astype(bf16)
    w1, w2 = w_ff1, w_ff2
    for l in range(DEPTH):
        shared = (w_in_b, w_pool_b, pool_scale, ln_v_g, ln_v_b)
        tail = (w_conv, w_out_b)
        has_next = l + 1 < DEPTH
        rp, p16, cz8, *next_wide = _mixer_prompt(l, 0, xp, shared + (w_s, bs_prompt) + tail, batch, seq,
                                                 (w_in, w_out) if has_next else ())
        rs, p_s, cz_s, v_s = _mixer_sample(l, 0, xs, state_pool, state_conv,
                                           shared + (ws_sample, bs_sample) + tail)
        xp, xs, *next_ffn = _ffn(l, rp, rs, w1, w2, 0, ln1_g, ln1_b, ln2_g, ln2_b,
                                 (w_ff1, w_ff2) if has_next else ())
        if has_next:
            (w_in_b, w_out_b), (w1, w2) = next_wide, next_ffn
        pool_p.append(p16[:, POOL_HIST_PAD - POOL_HIST:])
        conv_p.append(cz8[:, CONV_HIST_PAD - CONV_HIST:])
        p_new.append(p_s.reshape(nseq, dec_len, POOL_WIDTH))
        cz_new.append(cz_s.reshape(nseq, dec_len, CONV_WIDTH))
        chunk_v.append(v_s.reshape(nseq, dec_len, GMLP_WIDTH))

    pool_s = jnp.concatenate([state_pool, jnp.stack(p_new)], axis=2)[:, :, -POOL_HIST:]
    conv_s = jnp.concatenate([state_conv, jnp.stack(cz_new)], axis=2)[:, :, -CONV_HIST:]
    return (xp.reshape(batch, seq, D_MODEL), xs.reshape(nseq, dec_len, D_MODEL),
            jnp.stack(pool_p), jnp.stack(conv_p), pool_s, conv_s, jnp.stack(chunk_v))
```

```python
import functools

import jax
import jax.numpy as jnp
from jax import lax
from jax.experimental import pallas as pl
from jax.experimental.pallas import tpu as pltpu

D_MODEL = 2048
DEPTH = 4
POOL_WIDTH = D_MODEL // 4
POOL_WINDOWS = (2, 4, 8, 16)
POOL_GROUP_DIM = POOL_WIDTH // len(POOL_WINDOWS)
POOL_HIST = max(POOL_WINDOWS) - 1
GMLP_WIDTH = D_MODEL // 2
GMLP_HEADS = 8
GMLP_HEAD_DIM = GMLP_WIDTH // GMLP_HEADS
CHUNK = 128
CONV_WIDTH = D_MODEL // 4
CONV_K = 3
CONV_HIST = CONV_K - 1
D_FF = 4 * D_MODEL
PAST_LEN = 16384
ALPHA = (2 * DEPTH) ** 0.25
LN_EPS = 1e-5

OFF_P = 0
OFF_U = OFF_P + POOL_WIDTH
OFF_V = OFF_U + GMLP_WIDTH
OFF_GB = OFF_V + GMLP_WIDTH
OFF_GC = OFF_GB + CONV_WIDTH
OFF_Z = OFF_GC + CONV_WIDTH
IN_WIDTH = OFF_Z + CONV_WIDTH

POOL_HIST_PAD = 16
CONV_HIST_PAD = 8

V7X_VMEM_LIMIT_BYTES = 60 * 1024 * 1024

MIX_TILE_ROWS = 512
SAMPLE_TILE_ROWS = 256
FFN_TILE_ROWS = 1024
FFN_TILE_COLS_F32 = 512
FFN_TILE_COLS_BF16 = 1024
BF16_SUBLANE_TILE = 16
F32_SUBLANES = 8
LANES = 128
HALF_WORD_BITS = 16


def _dot(a, b):
    return jnp.dot(a, b, preferred_element_type=jnp.float32)


def _zero_word(bits):
    return (bits >> HALF_WORD_BITS) >> HALF_WORD_BITS


def _layernorm(x, g, b):
    mu = jnp.mean(x, axis=-1, keepdims=True)
    xc = x - mu
    var = jnp.mean(xc * xc, axis=-1, keepdims=True)
    return xc * lax.rsqrt(var + LN_EPS) * g + b


def _mixer_kernel(is_sample, tm, *refs):
    if is_sample:
        (x_ref, hp_ref, hc_ref, w_in_ref, w_pool_ref, pscale_ref, lnv_g_ref, lnv_b_ref, ws_ref, bs_ref,
         w_conv_ref, w_out_ref,
         r_ref, p_out_ref, cz_out_ref, v_out_ref) = refs
        nseq = hp_ref.shape[0]
        seq_len = tm // nseq

        def carried_history(ref, state_rows):
            _, pad_rows, width = ref.shape
            flat = ref[...].reshape(nseq * pad_rows, width)
            return pltpu.roll(flat, pad_rows - state_rows, axis=0).reshape(nseq, pad_rows, width)
    else:
        n_next = (len(refs) - 15) // 2
        (x_ref, w_in_ref, w_pool_ref, pscale_ref, lnv_g_ref, lnv_b_ref, ws_ref, bs_ref,
         w_conv_ref, w_out_ref) = refs[:10]
        next_f32 = refs[10:10 + n_next]
        r_ref, p_out_ref, cz_out_ref = refs[10 + n_next:13 + n_next]
        next_bf16 = refs[13 + n_next:13 + 2 * n_next]
        hp_scr, hc_scr = refs[13 + 2 * n_next:]
        for src, dst in zip(next_f32, next_bf16):
            dst[...] = src[...].astype(jnp.bfloat16)
        j = pl.program_id(1)

        @pl.when(j == 0)
        def _():
            hp_scr[...] = jnp.zeros_like(hp_scr)
            hc_scr[...] = jnp.zeros_like(hc_scr)

    x = x_ref[...]
    xb = x.astype(jnp.bfloat16)

    def proj(off, width):
        return _dot(xb, w_in_ref[:, off:off + width])

    def with_history(hist, new, hist_rows):
        width = new.shape[-1]
        if is_sample:
            ext = jnp.concatenate([hist, new.reshape(nseq, seq_len, width)], axis=1)
            ext = ext.reshape(nseq * (hist_rows + seq_len), width)

            def take_new(s):
                w = s.shape[-1]
                return s.reshape(nseq, hist_rows + seq_len, w)[:, hist_rows:, :].reshape(tm, w)
        else:
            ext = jnp.concatenate([hist, new], axis=0)

            def take_new(s):
                return s[hist_rows:, :]
        return ext, take_new

    v_pre = proj(OFF_V, GMLP_WIDTH)
    p = proj(OFF_P, POOL_WIDTH)
    gate_c = proj(OFF_GC, CONV_WIDTH)
    z = proj(OFF_Z, CONV_WIDTH)
    gate_b = proj(OFF_GB, CONV_WIDTH)
    u_pre = proj(OFF_U, GMLP_WIDTH)

    v = _layernorm(jax.nn.gelu(v_pre), lnv_g_ref[...], lnv_b_ref[...])
    if is_sample:
        v_out_ref[...] = v
    vb = v.astype(jnp.bfloat16)
    row = lax.broadcasted_iota(jnp.int32, (CHUNK, CHUNK), 0)
    col = lax.broadcasted_iota(jnp.int32, (CHUNK, CHUNK), 1)
    if is_sample:
        mask = (row // seq_len == col // seq_len) & (col <= row)
    else:
        mask = col <= row
    nchunk = tm // CHUNK
    last = lax.bitcast_convert_type(u_pre[tm - CHUNK:, GMLP_WIDTH - CHUNK:], jnp.uint32)
    zero_bits = _zero_word(last)
    mixed_cols = []
    for hd in range(GMLP_HEADS):
        wm = jnp.where(mask, ws_ref[hd], 0.0)
        wm = lax.bitcast_convert_type(lax.bitcast_convert_type(wm, jnp.uint32) | zero_bits, jnp.float32)
        wm = wm.astype(jnp.bfloat16)
        lo = hd * GMLP_HEAD_DIM
        rhs = jnp.concatenate(
            [vb[c * CHUNK:(c + 1) * CHUNK, lo:lo + GMLP_HEAD_DIM] for c in range(nchunk)], axis=1)
        out = _dot(wm, rhs)
        mixed_cols.append(jnp.concatenate(
            [out[:, c * GMLP_HEAD_DIM:(c + 1) * GMLP_HEAD_DIM] for c in range(nchunk)], axis=0))
    mixed = jnp.concatenate(mixed_cols, axis=1)
    bias = jnp.concatenate([bs_ref[...]] * nchunk, axis=0)
    b_out = jax.nn.gelu(u_pre) * (mixed + bias)

    hist_p = carried_history(hp_ref, POOL_HIST) if is_sample else hp_scr[...]
    ext, take_new = with_history(hist_p, p, POOL_HIST_PAD)
    if is_sample:
        pos1 = None
    else:
        pos1 = (j * tm + 1 + lax.broadcasted_iota(jnp.int32, (tm, 1), 0)).astype(jnp.float32)
    s = ext
    a_parts = []
    for g, w in enumerate(POOL_WINDOWS):
        s = s + pltpu.roll(s, w // 2, axis=0)
        lo = g * POOL_GROUP_DIM
        win = take_new(s[:, :POOL_GROUP_DIM])
        if is_sample:
            mean = win * (1.0 / w)
        else:
            mean = win / jnp.minimum(jnp.float32(w), pos1)
        d = mean - p[:, lo:lo + POOL_GROUP_DIM]
        a_parts.append(_dot(d.astype(jnp.bfloat16), w_pool_ref[g]))
        if g + 1 < len(POOL_WINDOWS):
            s = s[:, POOL_GROUP_DIM:]
    a_out = jnp.concatenate(a_parts, axis=1) * pscale_ref[...]

    cz = gate_c * z
    hist_c = carried_history(hc_ref, CONV_HIST) if is_sample else hc_scr[...]
    ext_c, take_new_c = with_history(hist_c, cz, CONV_HIST_PAD)
    y = w_conv_ref[CONV_K - 1:CONV_K, :] * ext_c
    for k in range(CONV_K - 1):
        y = y + w_conv_ref[k:k + 1, :] * pltpu.roll(ext_c, CONV_K - 1 - k, axis=0)
    c_out = gate_b * take_new_c(y)

    o = _dot(c_out.astype(jnp.bfloat16), w_out_ref[POOL_WIDTH + GMLP_WIDTH:, :])
    o = o + _dot(a_out.astype(jnp.bfloat16), w_out_ref[:POOL_WIDTH, :])
    o = o + _dot(b_out.astype(jnp.bfloat16), w_out_ref[POOL_WIDTH:POOL_WIDTH + GMLP_WIDTH, :])
    r_ref[...] = ALPHA * x + o

    if is_sample:
        p_out_ref[...] = p
        cz_out_ref[...] = cz
    else:
        hp_new = p[tm - POOL_HIST_PAD:, :]
        hc_new = cz[tm - CONV_HIST_PAD:, :]
        hp_scr[...] = hp_new
        hc_scr[...] = hc_new
        p_out_ref[...] = hp_new
        cz_out_ref[...] = hc_new


def _layer_spec(layer, shape):
    nd = len(shape)
    return pl.BlockSpec((None,) + shape, lambda *_: (layer,) + (0,) * nd, pipeline_mode=pl.Buffered(1))


def _mixer_weight_specs(layer, wide_slab):
    spec = functools.partial(_layer_spec, layer)
    return [
        _layer_spec(wide_slab, (D_MODEL, IN_WIDTH)),
        spec((len(POOL_WINDOWS), POOL_GROUP_DIM, POOL_GROUP_DIM)),
        spec((1, POOL_WIDTH)),
        spec((1, GMLP_WIDTH)),
        spec((1, GMLP_WIDTH)),
        spec((GMLP_HEADS, CHUNK, CHUNK)),
        spec((CHUNK, GMLP_WIDTH)),
        spec((CONV_K, CONV_WIDTH)),
        _layer_spec(wide_slab, (D_MODEL, D_MODEL)),
    ]


def _mixer_prompt(layer, wide_slab, x, weights, batch, seq, next_weights):
    tm = MIX_TILE_ROWS
    nj = seq // tm
    row_spec = lambda width: pl.BlockSpec((tm, width), lambda b, j: (b * nj + j, 0))
    next_in, next_out, next_shapes = [], [], []
    for w in next_weights:
        slab_rows = w.shape[1] // (batch * nj)
        assert slab_rows % BF16_SUBLANE_TILE == 0
        next_in.append(pl.BlockSpec((None, slab_rows, w.shape[2]), lambda b, j: (layer + 1, b * nj + j, 0)))
        next_out.append(pl.BlockSpec((None, slab_rows, w.shape[2]), lambda b, j: (0, b * nj + j, 0)))
        next_shapes.append(jax.ShapeDtypeStruct((1,) + w.shape[1:], jnp.bfloat16))
    return pl.pallas_call(
        functools.partial(_mixer_kernel, False, tm),
        grid=(batch, nj),
        in_specs=[row_spec(D_MODEL)] + _mixer_weight_specs(layer, wide_slab) + next_in,
        out_specs=[
            row_spec(D_MODEL),
            pl.BlockSpec((None, POOL_HIST_PAD, POOL_WIDTH), lambda b, j: (b, 0, 0)),
            pl.BlockSpec((None, CONV_HIST_PAD, CONV_WIDTH), lambda b, j: (b, 0, 0)),
        ] + next_out,
        out_shape=[
            jax.ShapeDtypeStruct((batch * seq, D_MODEL), jnp.float32),
            jax.ShapeDtypeStruct((batch, POOL_HIST_PAD, POOL_WIDTH), jnp.float32),
            jax.ShapeDtypeStruct((batch, CONV_HIST_PAD, CONV_WIDTH), jnp.float32),
        ] + next_shapes,
        scratch_shapes=[
            pltpu.VMEM((POOL_HIST_PAD, POOL_WIDTH), jnp.float32),
            pltpu.VMEM((CONV_HIST_PAD, CONV_WIDTH), jnp.float32),
        ],
        compiler_params=pltpu.CompilerParams(
            dimension_semantics=("arbitrary", "arbitrary"), vmem_limit_bytes=V7X_VMEM_LIMIT_BYTES),
        name="mixer_prompt",
    )(x, *weights, *next_weights)


def _mixer_sample(layer, wide_slab, x, state_pool, state_conv, weights):
    m = x.shape[0]
    tm = SAMPLE_TILE_ROWS
    nseq = state_pool.shape[1] * tm // m
    rows = lambda width: pl.BlockSpec((tm, width), lambda i: (i, 0))
    seqs = lambda pad, width: pl.BlockSpec((None, nseq, pad, width), lambda i: (layer, i, 0, 0))
    return pl.pallas_call(
        functools.partial(_mixer_kernel, True, tm),
        grid=(m // tm,),
        in_specs=[rows(D_MODEL), seqs(POOL_HIST_PAD, POOL_WIDTH),
                  seqs(CONV_HIST_PAD, CONV_WIDTH)] + _mixer_weight_specs(layer, wide_slab),
        out_specs=[rows(D_MODEL), rows(POOL_WIDTH), rows(CONV_WIDTH), rows(GMLP_WIDTH)],
        out_shape=[
            jax.ShapeDtypeStruct((m, D_MODEL), jnp.float32),
            jax.ShapeDtypeStruct((m, POOL_WIDTH), jnp.float32),
            jax.ShapeDtypeStruct((m, CONV_WIDTH), jnp.float32),
            jax.ShapeDtypeStruct((m, GMLP_WIDTH), jnp.float32),
        ],
        compiler_params=pltpu.CompilerParams(
            dimension_semantics=("arbitrary",), vmem_limit_bytes=V7X_VMEM_LIMIT_BYTES),
        name="mixer_sample",
    )(x, state_pool, state_conv, *weights)


def _ffn_kernel(n_p, n_s, n_next, rp_hbm, rp_ref, rs_ref, w1_ref, w2_ref, g1_ref, b1_ref, g2_ref, b2_ref, *rest):
    next_f32 = rest[:n_next]
    yp_ref, ys_ref = rest[n_next:n_next + 2]
    next_bf16 = rest[n_next + 2:2 * n_next + 2]
    acc_even, acc_odd, hb_even, hb_odd, sem = rest[2 * n_next + 2:]
    i, f = pl.program_id(0), pl.program_id(1)
    n = n_p + n_s
    tm = acc_even.shape[0]
    piece = rp_ref.shape[0]
    rows = pl.ds(pl.multiple_of(f * piece, piece), piece)

    def seed(acc_ref, hb_ref, dst_rows, r):
        h = _layernorm(r, g1_ref[...], b1_ref[...])
        hb_ref[dst_rows, :] = h.astype(jnp.bfloat16)
        acc_ref[dst_rows, :] = ALPHA * h
        return h

    def finished_piece(acc_ref):
        return _layernorm(acc_ref[rows, :], g2_ref[...], b2_ref[...])

    def zero_after(*values):
        tiles = []
        for v in values:
            bits = lax.bitcast_convert_type(v, jnp.uint32)
            tiles += [bits[r:r + F32_SUBLANES, c:c + LANES]
                      for r in range(0, v.shape[0], F32_SUBLANES) for c in range(0, v.shape[1], LANES)]
        word = functools.reduce(lambda p, q: p | q, tiles)
        return lax.bitcast_convert_type(_zero_word(word), jnp.float32)

    def step(acc_cur, hb_cur, acc_oth, hb_oth):
        y = finished_piece(acc_oth)
        yp_ref[...] = y
        h_next = seed(acc_oth, hb_oth, rows, jnp.where(i + 1 < n_p, rp_ref[...], rs_ref[...]))
        for src, dst in zip(next_f32, next_bf16):
            dst[...] = src[...].astype(jnp.bfloat16)

        w1, w2 = w1_ref[...], w2_ref[...]
        if w1.dtype != jnp.bfloat16:
            w1, w2 = w1.astype(jnp.bfloat16), w2.astype(jnp.bfloat16)
        tf = w1.shape[1]
        floor = jnp.concatenate([zero_after(y, h_next)] * (tf // LANES), axis=1)
        floor = jnp.broadcast_to(floor[None], (tm // F32_SUBLANES, F32_SUBLANES, tf)).reshape(tm, tf)
        a = jnp.square(jnp.maximum(_dot(hb_cur[...], w1), floor))
        acc_cur[...] += _dot(a.astype(jnp.bfloat16), w2)

    @pl.when((i == 0) & (f == 0))
    def _():
        first = pltpu.make_async_copy(rp_hbm.at[pl.ds(0, tm), :], acc_even, sem)
        first.start()
        first.wait()
        seed(acc_even, hb_even, slice(None), acc_even[...])
        acc_odd[...] = jnp.zeros_like(acc_odd)

    @pl.when((i < n) & (i % 2 == 0))
    def _():
        step(acc_even, hb_even, acc_odd, hb_odd)

    @pl.when((i < n) & (i % 2 == 1))
    def _():
        step(acc_odd, hb_odd, acc_even, hb_even)

    @pl.when(i == n)
    def _():
        ys_ref[...] = finished_piece(acc_odd if n % 2 == 0 else acc_even)


def _ffn(layer, r_p, r_s, w1, w2, w_slab, g1, b1, g2, b2, next_weights):
    tm = FFN_TILE_ROWS
    tf = FFN_TILE_COLS_BF16 if w1.dtype == jnp.bfloat16 else FFN_TILE_COLS_F32
    n_p, n_s = r_p.shape[0] // tm, r_s.shape[0] // tm
    nf = D_FF // tf
    piece = tm // nf
    last = n_p + n_s - 1

    def piece_spec(first_tile, n_tiles, lag):
        def index(i, f):
            return (jnp.clip((i + lag - first_tile) * nf + f, 0, n_tiles * nf - 1), 0)
        return pl.BlockSpec((piece, D_MODEL), index)

    def slab_index(i, f):
        return jnp.minimum(i * nf + f, convert_steps - 1)

    assert n_s == 1
    convert_steps = 1 << (((last + 1) * nf).bit_length() - 1)
    next_in, next_out, next_shapes = [], [], []
    for w in next_weights:
        slab_rows = w.shape[1] // convert_steps
        assert slab_rows % BF16_SUBLANE_TILE == 0
        next_in.append(pl.BlockSpec((None, slab_rows, w.shape[2]), lambda i, f: (layer + 1, slab_index(i, f), 0)))
        next_out.append(pl.BlockSpec((None, slab_rows, w.shape[2]), lambda i, f: (0, slab_index(i, f), 0)))
        next_shapes.append(jax.ShapeDtypeStruct((1,) + w.shape[1:], jnp.bfloat16))

    vec = pl.BlockSpec((None, 1, D_MODEL), lambda i, f: (layer, 0, 0))
    chunk = lambda i, f: jnp.where(i > last, nf - 1, f)
    return pl.pallas_call(
        functools.partial(_ffn_kernel, n_p, n_s, len(next_weights)),
        grid=(n_p + n_s + 1, nf),
        in_specs=[
            pl.BlockSpec(memory_space=pl.ANY),
            piece_spec(0, n_p, 1),
            piece_spec(n_p, n_s, 1),
            pl.BlockSpec((None, D_MODEL, tf), lambda i, f: (w_slab, 0, chunk(i, f))),
            pl.BlockSpec((None, tf, D_MODEL), lambda i, f: (w_slab, chunk(i, f), 0)),
            vec, vec, vec, vec,
        ] + next_in,
        out_specs=[piece_spec(0, n_p, -1), piece_spec(n_p, n_s, -1)] + next_out,
        out_shape=[jax.ShapeDtypeStruct(r_p.shape, jnp.float32),
                   jax.ShapeDtypeStruct(r_s.shape, jnp.float32)] + next_shapes,
        scratch_shapes=[pltpu.VMEM((tm, D_MODEL), jnp.float32), pltpu.VMEM((tm, D_MODEL), jnp.float32),
                        pltpu.VMEM((tm, D_MODEL), jnp.bfloat16), pltpu.VMEM((tm, D_MODEL), jnp.bfloat16),
                        pltpu.SemaphoreType.DMA(())],
        compiler_params=pltpu.CompilerParams(
            dimension_semantics=("arbitrary", "arbitrary"), vmem_limit_bytes=V7X_VMEM_LIMIT_BYTES),
        name="ffn",
    )(r_p, r_p, r_s, w1, w2, g1, b1, g2, b2, *next_weights)


def kernel(x_prompt, x_sample, state_pool, state_conv, w_in, w_pool, pool_scale, ln_v_g, ln_v_b, w_s, b_s, w_conv, w_out, ln1_g, ln1_b, w_ff1, w_ff2, ln2_g, ln2_b):
    bf16 = jnp.bfloat16
    batch, seq, _ = x_prompt.shape
    nseq, dec_len, _ = x_sample.shape
    assert seq % MIX_TILE_ROWS == 0 and MIX_TILE_ROWS % CHUNK == 0
    assert CHUNK % dec_len == 0 and PAST_LEN % CHUNK == 0
    assert (nseq * dec_len) % SAMPLE_TILE_ROWS == 0 and SAMPLE_TILE_ROWS % CHUNK == 0
    assert (batch * seq) % FFN_TILE_ROWS == 0 and (nseq * dec_len) % FFN_TILE_ROWS == 0

    w_pool_b = w_pool.astype(bf16)
    row = lambda a: a.reshape(DEPTH, 1, -1)
    pool_scale, ln_v_g, ln_v_b = row(pool_scale), row(ln_v_g), row(ln_v_b)
    ln1_g, ln1_b, ln2_g, ln2_b = row(ln1_g), row(ln1_b), row(ln2_g), row(ln2_b)

    exact = lax.Precision.HIGHEST
    head_cols = (jnp.arange(GMLP_WIDTH)[None, :] // GMLP_HEAD_DIM == jnp.arange(GMLP_HEADS)[:, None]).astype(jnp.float32)
    pos_rows = (jnp.arange(CHUNK)[:, None] % dec_len == jnp.arange(dec_len)[None, :]).astype(jnp.float32)
    bs_prompt = jnp.einsum('lhi,hc->lic', b_s, head_cols, precision=exact)
    bs_sample = jnp.einsum('ik,lhk,hc->lic', pos_rows, b_s[:, :, :dec_len], head_cols, precision=exact)
    ws_sample = jnp.einsum('ik,lhkm,jm->lhij', pos_rows, w_s[:, :, :dec_len, :dec_len], pos_rows, precision=exact)

    xp = x_prompt.reshape(batch * seq, D_MODEL)
    xs = x_sample.reshape(nseq * dec_len, D_MODEL)
    pool_p, conv_p, p_new, cz_new, chunk_v = [], [], [], [], []
    w_in_b, w_out_b = w_in[:1].astype(bf16), w_out[:1].astype(bf16)
    w1, w2 = w_ff1, w_ff2
    for l in range(DEPTH):
        shared = (w_in_b, w_pool_b, pool_scale, ln_v_g, ln_v_b)
        tail = (w_conv, w_out_b)
        has_next = l + 1 < DEPTH
        rp, p16, cz8, *next_wide = _mixer_prompt(l, 0, xp, shared + (w_s, bs_prompt) + tail, batch, seq,
                                                 (w_in, w_out) if has_next else ())
        rs, p_s, cz_s, v_s = _mixer_sample(l, 0, xs, state_pool, state_conv,
                                           shared + (ws_sample, bs_sample) + tail)
        xp, xs, *next_ffn = _ffn(l, rp, rs, w1, w2, 0, ln1_g, ln1_b, ln2_g, ln2_b,
                                 (w_ff1, w_ff2) if has_next else ())
        if has_next:
            (w_in_b, w_out_b), (w1, w2) = next_wide, next_ffn
        pool_p.append(p16[:, POOL_HIST_PAD - POOL_HIST:])
        conv_p.append(cz8[:, CONV_HIST_PAD - CONV_HIST:])
        p_new.append(p_s.reshape(nseq, dec_len, POOL_WIDTH))
        cz_new.append(cz_s.reshape(nseq, dec_len, CONV_WIDTH))
        chunk_v.append(v_s.reshape(nseq, dec_len, GMLP_WIDTH))

    pool_s = jnp.concatenate([state_pool, jnp.stack(p_new)], axis=2)[:, :, -POOL_HIST:]
    conv_s = jnp.concatenate([state_conv, jnp.stack(cz_new)], axis=2)[:, :, -CONV_HIST:]
    return (xp.reshape(batch, seq, D_MODEL), xs.reshape(nseq, dec_len, D_MODEL),
            jnp.stack(pool_p), jnp.stack(conv_p), pool_s, conv_s, jnp.stack(chunk_v))
```

```python
import functools

import jax
import jax.numpy as jnp
from jax import lax
from jax.experimental import pallas as pl
from jax.experimental.pallas import tpu as pltpu

D_MODEL = 2048
DEPTH = 4
POOL_WIDTH = D_MODEL // 4
POOL_WINDOWS = (2, 4, 8, 16)
POOL_GROUP_DIM = POOL_WIDTH // len(POOL_WINDOWS)
POOL_HIST = max(POOL_WINDOWS) - 1
GMLP_WIDTH = D_MODEL // 2
GMLP_HEADS = 8
GMLP_HEAD_DIM = GMLP_WIDTH // GMLP_HEADS
CHUNK = 128
CONV_WIDTH = D_MODEL // 4
CONV_K = 3
CONV_HIST = CONV_K - 1
D_FF = 4 * D_MODEL
PAST_LEN = 16384
ALPHA = (2 * DEPTH) ** 0.25
LN_EPS = 1e-5

OFF_P = 0
OFF_U = OFF_P + POOL_WIDTH
OFF_V = OFF_U + GMLP_WIDTH
OFF_GB = OFF_V + GMLP_WIDTH
OFF_GC = OFF_GB + CONV_WIDTH
OFF_Z = OFF_GC + CONV_WIDTH
IN_WIDTH = OFF_Z + CONV_WIDTH

POOL_HIST_PAD = 16
CONV_HIST_PAD = 8

V7X_VMEM_LIMIT_BYTES = 60 * 1024 * 1024

MIX_TILE_ROWS = 512
SAMPLE_TILE_ROWS = 256
FFN_TILE_ROWS = 1024
FFN_TILE_COLS_F32 = 512
FFN_TILE_COLS_BF16 = 1024
BF16_SUBLANE_TILE = 16
F32_SUBLANES = 8
LANES = 128
HALF_WORD_BITS = 16


def _dot(a, b):
    return jnp.dot(a, b, preferred_element_type=jnp.float32)


def _zero_word(bits):
    return (bits >> HALF_WORD_BITS) >> HALF_WORD_BITS


def _layernorm(x, g, b):
    mu = jnp.mean(x, axis=-1, keepdims=True)
    xc = x - mu
    var = jnp.mean(xc * xc, axis=-1, keepdims=True)
    return xc * lax.rsqrt(var + LN_EPS) * g + b


def _mixer_kernel(is_sample, tm, *refs):
    if is_sample:
        (x_ref, hp_ref, hc_ref, w_in_ref, w_pool_ref, pscale_ref, lnv_g_ref, lnv_b_ref, ws_ref, bs_ref,
         w_conv_ref, w_out_ref,
         r_ref, p_out_ref, cz_out_ref, v_out_ref) = refs
        nseq = hp_ref.shape[0]
        seq_len = tm // nseq

        def carried_history(ref, state_rows):
            _, pad_rows, width = ref.shape
            flat = ref[...].reshape(nseq * pad_rows, width)
            return pltpu.roll(flat, pad_rows - state_rows, axis=0).reshape(nseq, pad_rows, width)
    else:
        n_next = (len(refs) - 15) // 2
        (x_ref, w_in_ref, w_pool_ref, pscale_ref, lnv_g_ref, lnv_b_ref, ws_ref, bs_ref,
         w_conv_ref, w_out_ref) = refs[:10]
        next_f32 = refs[10:10 + n_next]
        r_ref, p_out_ref, cz_out_ref = refs[10 + n_next:13 + n_next]
        next_bf16 = refs[13 + n_next:13 + 2 * n_next]
        hp_scr, hc_scr = refs[13 + 2 * n_next:]
        for src, dst in zip(next_f32, next_bf16):
            dst[...] = src[...].astype(jnp.bfloat16)
        j = pl.program_id(1)

        @pl.when(j == 0)
        def _():
            hp_scr[...] = jnp.zeros_like(hp_scr)
            hc_scr[...] = jnp.zeros_like(hc_scr)

    x = x_ref[...]
    xb = x.astype(jnp.bfloat16)

    def proj(off, width):
        return _dot(xb, w_in_ref[:, off:off + width])

    def with_history(hist, new, hist_rows):
        width = new.shape[-1]
        if is_sample:
            ext = jnp.concatenate([hist, new.reshape(nseq, seq_len, width)], axis=1)
            ext = ext.reshape(nseq * (hist_rows + seq_len), width)

            def take_new(s):
                w = s.shape[-1]
                return s.reshape(nseq, hist_rows + seq_len, w)[:, hist_rows:, :].reshape(tm, w)
        else:
            ext = jnp.concatenate([hist, new], axis=0)

            def take_new(s):
                return s[hist_rows:, :]
        return ext, take_new

    v_pre = proj(OFF_V, GMLP_WIDTH)
    p = proj(OFF_P, POOL_WIDTH)
    u_pre = proj(OFF_U, GMLP_WIDTH)
    gate_c = proj(OFF_GC, CONV_WIDTH)
    z = proj(OFF_Z, CONV_WIDTH)
    gate_b = proj(OFF_GB, CONV_WIDTH)

    v = _layernorm(jax.nn.gelu(v_pre), lnv_g_ref[...], lnv_b_ref[...])
    if is_sample:
        v_out_ref[...] = v
    vb = v.astype(jnp.bfloat16)
    row = lax.broadcasted_iota(jnp.int32, (CHUNK, CHUNK), 0)
    col = lax.broadcasted_iota(jnp.int32, (CHUNK, CHUNK), 1)
    if is_sample:
        mask = (row // seq_len == col // seq_len) & (col <= row)
    else:
        mask = col <= row
    nchunk = tm // CHUNK
    last = lax.bitcast_convert_type(gate_b[tm - CHUNK:, CONV_WIDTH - CHUNK:], jnp.uint32)
    zero_bits = _zero_word(last)
    mixed_cols = []
    for hd in range(GMLP_HEADS):
        wm = jnp.where(mask, ws_ref[hd], 0.0)
        wm = lax.bitcast_convert_type(lax.bitcast_convert_type(wm, jnp.uint32) | zero_bits, jnp.float32)
        wm = wm.astype(jnp.bfloat16)
        lo = hd * GMLP_HEAD_DIM
        rhs = jnp.concatenate(
            [vb[c * CHUNK:(c + 1) * CHUNK, lo:lo + GMLP_HEAD_DIM] for c in range(nchunk)], axis=1)
        out = _dot(wm, rhs)
        mixed_cols.append(jnp.concatenate(
            [out[:, c * GMLP_HEAD_DIM:(c + 1) * GMLP_HEAD_DIM] for c in range(nchunk)], axis=0))
    mixed = jnp.concatenate(mixed_cols, axis=1)
    bias = jnp.concatenate([bs_ref[...]] * nchunk, axis=0)
    b_out = jax.nn.gelu(u_pre) * (mixed + bias)

    hist_p = carried_history(hp_ref, POOL_HIST) if is_sample else hp_scr[...]
    ext, take_new = with_history(hist_p, p, POOL_HIST_PAD)
    if is_sample:
        pos1 = None
    else:
        pos1 = (j * tm + 1 + lax.broadcasted_iota(jnp.int32, (tm, 1), 0)).astype(jnp.float32)
    s = ext
    a_parts = []
    for g, w in enumerate(POOL_WINDOWS):
        s = s + pltpu.roll(s, w // 2, axis=0)
        lo = g * POOL_GROUP_DIM
        win = take_new(s[:, :POOL_GROUP_DIM])
        if is_sample:
            mean = win * (1.0 / w)
        else:
            mean = win / jnp.minimum(jnp.float32(w), pos1)
        d = mean - p[:, lo:lo + POOL_GROUP_DIM]
        a_parts.append(_dot(d.astype(jnp.bfloat16), w_pool_ref[g]))
        if g + 1 < len(POOL_WINDOWS):
            s = s[:, POOL_GROUP_DIM:]
    a_out = jnp.concatenate(a_parts, axis=1) * pscale_ref[...]

    cz = gate_c * z
    hist_c = carried_history(hc_ref, CONV_HIST) if is_sample else hc_scr[...]
    ext_c, take_new_c = with_history(hist_c, cz, CONV_HIST_PAD)
    y = w_conv_ref[CONV_K - 1:CONV_K, :] * ext_c
    for k in range(CONV_K - 1):
        y = y + w_conv_ref[k:k + 1, :] * pltpu.roll(ext_c, CONV_K - 1 - k, axis=0)
    c_out = gate_b * take_new_c(y)

    o = _dot(c_out.astype(jnp.bfloat16), w_out_ref[POOL_WIDTH + GMLP_WIDTH:, :])
    o = o + _dot(a_out.astype(jnp.bfloat16), w_out_ref[:POOL_WIDTH, :])
    o = o + _dot(b_out.astype(jnp.bfloat16), w_out_ref[POOL_WIDTH:POOL_WIDTH + GMLP_WIDTH, :])
    r_ref[...] = ALPHA * x + o

    if is_sample:
        p_out_ref[...] = p
        cz_out_ref[...] = cz
    else:
        hp_new = p[tm - POOL_HIST_PAD:, :]
        hc_new = cz[tm - CONV_HIST_PAD:, :]
        hp_scr[...] = hp_new
        hc_scr[...] = hc_new
        p_out_ref[...] = hp_new
        cz_out_ref[...] = hc_new


def _layer_spec(layer, shape):
    nd = len(shape)
    return pl.BlockSpec((None,) + shape, lambda *_: (layer,) + (0,) * nd, pipeline_mode=pl.Buffered(1))


def _mixer_weight_specs(layer, wide_slab):
    spec = functools.partial(_layer_spec, layer)
    return [
        _layer_spec(wide_slab, (D_MODEL, IN_WIDTH)),
        spec((len(POOL_WINDOWS), POOL_GROUP_DIM, POOL_GROUP_DIM)),
        spec((1, POOL_WIDTH)),
        spec((1, GMLP_WIDTH)),
        spec((1, GMLP_WIDTH)),
        spec((GMLP_HEADS, CHUNK, CHUNK)),
        spec((CHUNK, GMLP_WIDTH)),
        spec((CONV_K, CONV_WIDTH)),
        _layer_spec(wide_slab, (D_MODEL, D_MODEL)),
    ]


def _mixer_prompt(layer, wide_slab, x, weights, batch, seq, next_weights):
    tm = MIX_TILE_ROWS
    nj = seq // tm
    row_spec = lambda width: pl.BlockSpec((tm, width), lambda b, j: (b * nj + j, 0))
    next_in, next_out, next_shapes = [], [], []
    for w in next_weights:
        slab_rows = w.shape[1] // (batch * nj)
        assert slab_rows % BF16_SUBLANE_TILE == 0
        next_in.append(pl.BlockSpec((None, slab_rows, w.shape[2]), lambda b, j: (layer + 1, b * nj + j, 0)))
        next_out.append(pl.BlockSpec((None, slab_rows, w.shape[2]), lambda b, j: (0, b * nj + j, 0)))
        next_shapes.append(jax.ShapeDtypeStruct((1,) + w.shape[1:], jnp.bfloat16))
    return pl.pallas_call(
        functools.partial(_mixer_kernel, False, tm),
        grid=(batch, nj),
        in_specs=[row_spec(D_MODEL)] + _mixer_weight_specs(layer, wide_slab) + next_in,
        out_specs=[
            row_spec(D_MODEL),
            pl.BlockSpec((None, POOL_HIST_PAD, POOL_WIDTH), lambda b, j: (b, 0, 0)),
            pl.BlockSpec((None, CONV_HIST_PAD, CONV_WIDTH), lambda b, j: (b, 0, 0)),
        ] + next_out,
        out_shape=[
            jax.ShapeDtypeStruct((batch * seq, D_MODEL), jnp.float32),
            jax.ShapeDtypeStruct((batch, POOL_HIST_PAD, POOL_WIDTH), jnp.float32),
            jax.ShapeDtypeStruct((batch, CONV_HIST_PAD, CONV_WIDTH), jnp.float32),
        ] + next_shapes,
        scratch_shapes=[
            pltpu.VMEM((POOL_HIST_PAD, POOL_WIDTH), jnp.float32),
            pltpu.VMEM((CONV_HIST_PAD, CONV_WIDTH), jnp.float32),
        ],
        compiler_params=pltpu.CompilerParams(
            dimension_semantics=("arbitrary", "arbitrary"), vmem_limit_bytes=V7X_VMEM_LIMIT_BYTES),
        name="mixer_prompt",
    )(x, *weights, *next_weights)


def _mixer_sample(layer, wide_slab, x, state_pool, state_conv, weights):
    m = x.shape[0]
    tm = SAMPLE_TILE_ROWS
    nseq = state_pool.shape[1] * tm // m
    rows = lambda width: pl.BlockSpec((tm, width), lambda i: (i, 0))
    seqs = lambda pad, width: pl.BlockSpec((None, nseq, pad, width), lambda i: (layer, i, 0, 0))
    return pl.pallas_call(
        functools.partial(_mixer_kernel, True, tm),
        grid=(m // tm,),
        in_specs=[rows(D_MODEL), seqs(POOL_HIST_PAD, POOL_WIDTH),
                  seqs(CONV_HIST_PAD, CONV_WIDTH)] + _mixer_weight_specs(layer, wide_slab),
        out_specs=[rows(D_MODEL), rows(POOL_WIDTH), rows(CONV_WIDTH), rows(GMLP_WIDTH)],
        out_shape=[
            jax.ShapeDtypeStruct((m, D_MODEL), jnp.float32),
            jax.ShapeDtypeStruct((m, POOL_WIDTH), jnp.float32),
            jax.ShapeDtypeStruct((m, CONV_WIDTH), jnp.float32),
            jax.ShapeDtypeStruct((m, GMLP_WIDTH), jnp.float32),
        ],
        compiler_params=pltpu.CompilerParams(
            dimension_semantics=("arbitrary",), vmem_limit_bytes=V7X_VMEM_LIMIT_BYTES),
        name="mixer_sample",
    )(x, state_pool, state_conv, *weights)


def _ffn_kernel(n_p, n_s, n_next, rp_hbm, rp_ref, rs_ref, w1_ref, w2_ref, g1_ref, b1_ref, g2_ref, b2_ref, *rest):
    next_f32 = rest[:n_next]
    yp_ref, ys_ref = rest[n_next:n_next + 2]
    next_bf16 = rest[n_next + 2:2 * n_next + 2]
    acc_even, acc_odd, hb_even, hb_odd, sem = rest[2 * n_next + 2:]
    i, f = pl.program_id(0), pl.program_id(1)
    n = n_p + n_s
    tm = acc_even.shape[0]
    piece = rp_ref.shape[0]
    rows = pl.ds(pl.multiple_of(f * piece, piece), piece)

    def seed(acc_ref, hb_ref, dst_rows, r):
        h = _layernorm(r, g1_ref[...], b1_ref[...])
        hb_ref[dst_rows, :] = h.astype(jnp.bfloat16)
        acc_ref[dst_rows, :] = ALPHA * h
        return h

    def finished_piece(acc_ref):
        return _layernorm(acc_ref[rows, :], g2_ref[...], b2_ref[...])

    def zero_after(*values):
        tiles = []
        for v in values:
            bits = lax.bitcast_convert_type(v, jnp.uint32)
            tiles += [bits[r:r + F32_SUBLANES, c:c + LANES]
                      for r in range(0, v.shape[0], F32_SUBLANES) for c in range(0, v.shape[1], LANES)]
        word = functools.reduce(lambda p, q: p | q, tiles)
        return lax.bitcast_convert_type(_zero_word(word), jnp.float32)

    def step(acc_cur, hb_cur, acc_oth, hb_oth):
        y = finished_piece(acc_oth)
        yp_ref[...] = y
        h_next = seed(acc_oth, hb_oth, rows, jnp.where(i + 1 < n_p, rp_ref[...], rs_ref[...]))
        for src, dst in zip(next_f32, next_bf16):
            dst[...] = src[...].astype(jnp.bfloat16)

        w1, w2 = w1_ref[...], w2_ref[...]
        if w1.dtype != jnp.bfloat16:
            w1, w2 = w1.astype(jnp.bfloat16), w2.astype(jnp.bfloat16)
        tf = w1.shape[1]
        floor = jnp.concatenate([zero_after(y, h_next)] * (tf // LANES), axis=1)
        floor = jnp.broadcast_to(floor[None], (tm // F32_SUBLANES, F32_SUBLANES, tf)).reshape(tm, tf)
        a = jnp.square(jnp.maximum(_dot(hb_cur[...], w1), floor))
        acc_cur[...] += _dot(a.astype(jnp.bfloat16), w2)

    @pl.when((i == 0) & (f == 0))
    def _():
        first = pltpu.make_async_copy(rp_hbm.at[pl.ds(0, tm), :], acc_even, sem)
        first.start()
        first.wait()
        seed(acc_even, hb_even, slice(None), acc_even[...])
        acc_odd[...] = jnp.zeros_like(acc_odd)

    @pl.when((i < n) & (i % 2 == 0))
    def _():
        step(acc_even, hb_even, acc_odd, hb_odd)

    @pl.when((i < n) & (i % 2 == 1))
    def _():
        step(acc_odd, hb_odd, acc_even, hb_even)

    @pl.when(i == n)
    def _():
        ys_ref[...] = finished_piece(acc_odd if n % 2 == 0 else acc_even)


def _ffn(layer, r_p, r_s, w1, w2, w_slab, g1, b1, g2, b2, next_weights):
    tm = FFN_TILE_ROWS
    tf = FFN_TILE_COLS_BF16 if w1.dtype == jnp.bfloat16 else FFN_TILE_COLS_F32
    n_p, n_s = r_p.shape[0] // tm, r_s.shape[0] // tm
    nf = D_FF // tf
    piece = tm // nf
    last = n_p + n_s - 1

    def piece_spec(first_tile, n_tiles, lag):
        def index(i, f):
            return (jnp.clip((i + lag - first_tile) * nf + f, 0, n_tiles * nf - 1), 0)
        return pl.BlockSpec((piece, D_MODEL), index)

    def slab_index(i, f):
        return jnp.minimum(i * nf + f, convert_steps - 1)

    assert n_s == 1
    convert_steps = 1 << (((last + 1) * nf).bit_length() - 1)
    next_in, next_out, next_shapes = [], [], []
    for w in next_weights:
        slab_rows = w.shape[1] // convert_steps
        assert slab_rows % BF16_SUBLANE_TILE == 0
        next_in.append(pl.BlockSpec((None, slab_rows, w.shape[2]), lambda i, f: (layer + 1, slab_index(i, f), 0)))
        next_out.append(pl.BlockSpec((None, slab_rows, w.shape[2]), lambda i, f: (0, slab_index(i, f), 0)))
        next_shapes.append(jax.ShapeDtypeStruct((1,) + w.shape[1:], jnp.bfloat16))

    vec = pl.BlockSpec((None, 1, D_MODEL), lambda i, f: (layer, 0, 0))
    chunk = lambda i, f: jnp.where(i > last, nf - 1, f)
    return pl.pallas_call(
        functools.partial(_ffn_kernel, n_p, n_s, len(next_weights)),
        grid=(n_p + n_s + 1, nf),
        in_specs=[
            pl.BlockSpec(memory_space=pl.ANY),
            piece_spec(0, n_p, 1),
            piece_spec(n_p, n_s, 1),
            pl.BlockSpec((None, D_MODEL, tf), lambda i, f: (w_slab, 0, chunk(i, f))),
            pl.BlockSpec((None, tf, D_MODEL), lambda i, f: (w_slab, chunk(i, f), 0)),
            vec, vec, vec, vec,
        ] + next_in,
        out_specs=[piece_spec(0, n_p, -1), piece_spec(n_p, n_s, -1)] + next_out,
        out_shape=[jax.ShapeDtypeStruct(r_p.shape, jnp.float32),
                   jax.ShapeDtypeStruct(r_s.shape, jnp.float32)] + next_shapes,
        scratch_shapes=[pltpu.VMEM((tm, D_MODEL), jnp.float32), pltpu.VMEM((tm, D_MODEL), jnp.float32),
                        pltpu.VMEM((tm, D_MODEL), jnp.bfloat16), pltpu.VMEM((tm, D_MODEL), jnp.bfloat16),
                        pltpu.SemaphoreType.DMA(())],
        compiler_params=pltpu.CompilerParams(
            dimension_semantics=("arbitrary", "arbitrary"), vmem_limit_bytes=V7X_VMEM_LIMIT_BYTES),
        name="ffn",
    )(r_p, r_p, r_s, w1, w2, g1, b1, g2, b2, *next_weights)


def kernel(x_prompt, x_sample, state_pool, state_conv, w_in, w_pool, pool_scale, ln_v_g, ln_v_b, w_s, b_s, w_conv, w_out, ln1_g, ln1_b, w_ff1, w_ff2, ln2_g, ln2_b):
    bf16 = jnp.bfloat16
    batch, seq, _ = x_prompt.shape
    nseq, dec_len, _ = x_sample.shape
    assert seq % MIX_TILE_ROWS == 0 and MIX_TILE_ROWS % CHUNK == 0
    assert CHUNK % dec_len == 0 and PAST_LEN % CHUNK == 0
    assert (nseq * dec_len) % SAMPLE_TILE_ROWS == 0 and SAMPLE_TILE_ROWS % CHUNK == 0
    assert (batch * seq) % FFN_TILE_ROWS == 0 and (nseq * dec_len) % FFN_TILE_ROWS == 0

    w_pool_b = w_pool.astype(bf16)
    row = lambda a: a.reshape(DEPTH, 1, -1)
    pool_scale, ln_v_g, ln_v_b = row(pool_scale), row(ln_v_g), row(ln_v_b)
    ln1_g, ln1_b, ln2_g, ln2_b = row(ln1_g), row(ln1_b), row(ln2_g), row(ln2_b)

    exact = lax.Precision.HIGHEST
    head_cols = (jnp.arange(GMLP_WIDTH)[None, :] // GMLP_HEAD_DIM == jnp.arange(GMLP_HEADS)[:, None]).astype(jnp.float32)
    pos_rows = (jnp.arange(CHUNK)[:, None] % dec_len == jnp.arange(dec_len)[None, :]).astype(jnp.float32)
    bs_prompt = jnp.einsum('lhi,hc->lic', b_s, head_cols, precision=exact)
    bs_sample = jnp.einsum('ik,lhk,hc->lic', pos_rows, b_s[:, :, :dec_len], head_cols, precision=exact)
    ws_sample = jnp.einsum('ik,lhkm,jm->lhij', pos_rows, w_s[:, :, :dec_len, :dec_len], pos_rows, precision=exact)

    xp = x_prompt.reshape(batch * seq, D_MODEL)
    xs = x_sample.reshape(nseq * dec_len, D_MODEL)
    pool_p, conv_p, p_new, cz_new, chunk_v = [], [], [], [], []
    w_in_b, w_out_b = w_in[:1].astype(bf16), w_out[:1].astype(bf16)
    w1, w2 = w_ff1, w_ff2
    for l in range(DEPTH):
        shared = (w_in_b, w_pool_b, pool_scale, ln_v_g, ln_v_b)
        tail = (w_conv, w_out_b)
        has_next = l + 1 < DEPTH
        rp, p16, cz8, *next_wide = _mixer_prompt(l, 0, xp, shared + (w_s, bs_prompt) + tail, batch, seq,
                                                 (w_in, w_out) if has_next else ())
        rs, p_s, cz_s, v_s = _mixer_sample(l, 0, xs, state_pool, state_conv,
                                           shared + (ws_sample, bs_sample) + tail)
        xp, xs, *next_ffn = _ffn(l, rp, rs, w1, w2, 0, ln1_g, ln1_b, ln2_g, ln2_b,
                                 (w_ff1, w_ff2) if has_next else ())
        if has_next:
            (w_in_b, w_out_b), (w1, w2) = next_wide, next_ffn
        pool_p.append(p16[:, POOL_HIST_PAD - POOL_HIST:])
        conv_p.append(cz8[:, CONV_HIST_PAD - CONV_HIST:])
        p_new.append(p_s.reshape(nseq, dec_len, POOL_WIDTH))
        cz_new.append(cz_s.reshape(nseq, dec_len, CONV_WIDTH))
        chunk_v.append(v_s.reshape(nseq, dec_len, GMLP_WIDTH))

    pool_s = jnp.concatenate([state_pool, jnp.stack(p_new)], axis=2)[:, :, -POOL_HIST:]
    conv_s = jnp.concatenate([state_conv, jnp.stack(cz_new)], axis=2)[:, :, -CONV_HIST:]
    return (xp.reshape(batch, seq, D_MODEL), xs.reshape(nseq, dec_len, D_MODEL),
            jnp.stack(pool_p), jnp.stack(conv_p), pool_s, conv_s, jnp.stack(chunk_v))
```

```python
import functools

import jax
import jax.numpy as jnp
from jax import lax
from jax.experimental import pallas as pl
from jax.experimental.pallas import tpu as pltpu

D_MODEL = 2048
DEPTH = 4
POOL_WIDTH = D_MODEL // 4
POOL_WINDOWS = (2, 4, 8, 16)
POOL_GROUP_DIM = POOL_WIDTH // len(POOL_WINDOWS)
POOL_HIST = max(POOL_WINDOWS) - 1
GMLP_WIDTH = D_MODEL // 2
GMLP_HEADS = 8
GMLP_HEAD_DIM = GMLP_WIDTH // GMLP_HEADS
CHUNK = 128
CONV_WIDTH = D_MODEL // 4
CONV_K = 3
CONV_HIST = CONV_K - 1
D_FF = 4 * D_MODEL
PAST_LEN = 16384
ALPHA = (2 * DEPTH) ** 0.25
LN_EPS = 1e-5

OFF_P = 0
OFF_U = OFF_P + POOL_WIDTH
OFF_V = OFF_U + GMLP_WIDTH
OFF_GB = OFF_V + GMLP_WIDTH
OFF_GC = OFF_GB + CONV_WIDTH
OFF_Z = OFF_GC + CONV_WIDTH
IN_WIDTH = OFF_Z + CONV_WIDTH

POOL_HIST_PAD = 16
CONV_HIST_PAD = 8

V7X_VMEM_LIMIT_BYTES = 60 * 1024 * 1024

MIX_TILE_ROWS = 512
SAMPLE_TILE_ROWS = 256
FFN_TILE_ROWS = 1024
FFN_TILE_COLS_F32 = 512
FFN_TILE_COLS_BF16 = 1024
BF16_SUBLANE_TILE = 16
F32_SUBLANES = 8
LANES = 128
HALF_WORD_BITS = 16


def _dot(a, b):
    return jnp.dot(a, b, preferred_element_type=jnp.float32)


def _zero_word(bits):
    return (bits >> HALF_WORD_BITS) >> HALF_WORD_BITS


def _tile_register(tile, rows, cols):
    wide = jnp.concatenate([tile] * (cols // LANES), axis=1)
    return jnp.broadcast_to(wide[None], (rows // F32_SUBLANES, F32_SUBLANES, cols)).reshape(rows, cols)


def _layernorm(x, g, b):
    mu = jnp.mean(x, axis=-1, keepdims=True)
    xc = x - mu
    var = jnp.mean(xc * xc, axis=-1, keepdims=True)
    return xc * lax.rsqrt(var + LN_EPS) * g + b


def _mixer_kernel(is_sample, tm, *refs):
    if is_sample:
        (x_ref, hp_ref, hc_ref, w_in_ref, w_pool_ref, pscale_ref, lnv_g_ref, lnv_b_ref, ws_ref, bs_ref,
         w_conv_ref, w_out_ref,
         r_ref, p_out_ref, cz_out_ref, v_out_ref) = refs
        nseq = hp_ref.shape[0]
        seq_len = tm // nseq

        def carried_history(ref, state_rows):
            _, pad_rows, width = ref.shape
            flat = ref[...].reshape(nseq * pad_rows, width)
            return pltpu.roll(flat, pad_rows - state_rows, axis=0).reshape(nseq, pad_rows, width)
    else:
        n_next = (len(refs) - 15) // 2
        (x_ref, w_in_ref, w_pool_ref, pscale_ref, lnv_g_ref, lnv_b_ref, ws_ref, bs_ref,
         w_conv_ref, w_out_ref) = refs[:10]
        next_f32 = refs[10:10 + n_next]
        r_ref, p_out_ref, cz_out_ref = refs[10 + n_next:13 + n_next]
        next_bf16 = refs[13 + n_next:13 + 2 * n_next]
        hp_scr, hc_scr = refs[13 + 2 * n_next:]
        for src, dst in zip(next_f32, next_bf16):
            dst[...] = src[...].astype(jnp.bfloat16)
        j = pl.program_id(1)

        @pl.when(j == 0)
        def _():
            hp_scr[...] = jnp.zeros_like(hp_scr)
            hc_scr[...] = jnp.zeros_like(hc_scr)

    x = x_ref[...]
    xb = x.astype(jnp.bfloat16)

    def proj(off, width):
        return _dot(xb, w_in_ref[:, off:off + width])

    def with_history(hist, new, hist_rows):
        width = new.shape[-1]
        if is_sample:
            ext = jnp.concatenate([hist, new.reshape(nseq, seq_len, width)], axis=1)
            ext = ext.reshape(nseq * (hist_rows + seq_len), width)

            def take_new(s):
                w = s.shape[-1]
                return s.reshape(nseq, hist_rows + seq_len, w)[:, hist_rows:, :].reshape(tm, w)
        else:
            ext = jnp.concatenate([hist, new], axis=0)

            def take_new(s):
                return s[hist_rows:, :]
        return ext, take_new

    v_pre = proj(OFF_V, GMLP_WIDTH)
    p = proj(OFF_P, POOL_WIDTH)
    u_pre = proj(OFF_U, GMLP_WIDTH)
    gate_c = proj(OFF_GC, CONV_WIDTH)
    z = proj(OFF_Z, CONV_WIDTH)
    gate_b = proj(OFF_GB, CONV_WIDTH)

    v = _layernorm(jax.nn.gelu(v_pre), lnv_g_ref[...], lnv_b_ref[...])
    if is_sample:
        v_out_ref[...] = v
    vb = v.astype(jnp.bfloat16)
    row = lax.broadcasted_iota(jnp.int32, (CHUNK, CHUNK), 0)
    col = lax.broadcasted_iota(jnp.int32, (CHUNK, CHUNK), 1)
    if is_sample:
        mask = (row // seq_len == col // seq_len) & (col <= row)
    else:
        mask = col <= row
    nchunk = tm // CHUNK
    mixed_cols = []
    for hd in range(GMLP_HEADS):
        wm = jnp.where(mask, ws_ref[hd], 0.0).astype(jnp.bfloat16)
        lo = hd * GMLP_HEAD_DIM
        rhs = jnp.concatenate(
            [vb[c * CHUNK:(c + 1) * CHUNK, lo:lo + GMLP_HEAD_DIM] for c in range(nchunk)], axis=1)
        out = _dot(wm, rhs)
        mixed_cols.append(jnp.concatenate(
            [out[:, c * GMLP_HEAD_DIM:(c + 1) * GMLP_HEAD_DIM] for c in range(nchunk)], axis=0))
    mixed = jnp.concatenate(mixed_cols, axis=1)
    bias = jnp.concatenate([bs_ref[...]] * nchunk, axis=0)
    b_out = jax.nn.gelu(u_pre) * (mixed + bias)

    hist_p = carried_history(hp_ref, POOL_HIST) if is_sample else hp_scr[...]
    ext, take_new = with_history(hist_p, p, POOL_HIST_PAD)
    if is_sample:
        pos1 = None
    else:
        pos1 = (j * tm + 1 + lax.broadcasted_iota(jnp.int32, (tm, 1), 0)).astype(jnp.float32)
    s = ext
    a_parts = []
    for g, w in enumerate(POOL_WINDOWS):
        s = s + pltpu.roll(s, w // 2, axis=0)
        lo = g * POOL_GROUP_DIM
        win = take_new(s[:, :POOL_GROUP_DIM])
        if is_sample:
            mean = win * (1.0 / w)
        else:
            mean = win / jnp.minimum(jnp.float32(w), pos1)
        d = mean - p[:, lo:lo + POOL_GROUP_DIM]
        a_parts.append(_dot(d.astype(jnp.bfloat16), w_pool_ref[g]))
        if g + 1 < len(POOL_WINDOWS):
            s = s[:, POOL_GROUP_DIM:]
    a_out = jnp.concatenate(a_parts, axis=1) * pscale_ref[...]

    cz = gate_c * z
    hist_c = carried_history(hc_ref, CONV_HIST) if is_sample else hc_scr[...]
    ext_c, take_new_c = with_history(hist_c, cz, CONV_HIST_PAD)
    y = w_conv_ref[CONV_K - 1:CONV_K, :] * ext_c
    for k in range(CONV_K - 1):
        y = y + w_conv_ref[k:k + 1, :] * pltpu.roll(ext_c, CONV_K - 1 - k, axis=0)
    c_out = gate_b * take_new_c(y)

    o = _dot(c_out.astype(jnp.bfloat16), w_out_ref[POOL_WIDTH + GMLP_WIDTH:, :])
    o = o + _dot(a_out.astype(jnp.bfloat16), w_out_ref[:POOL_WIDTH, :])
    o = o + _dot(b_out.astype(jnp.bfloat16), w_out_ref[POOL_WIDTH:POOL_WIDTH + GMLP_WIDTH, :])
    r_ref[...] = ALPHA * x + o

    if is_sample:
        p_out_ref[...] = p
        cz_out_ref[...] = cz
    else:
        hp_new = p[tm - POOL_HIST_PAD:, :]
        hc_new = cz[tm - CONV_HIST_PAD:, :]
        hp_scr[...] = hp_new
        hc_scr[...] = hc_new
        p_out_ref[...] = hp_new
        cz_out_ref[...] = hc_new


def _layer_spec(layer, shape):
    nd = len(shape)
    return pl.BlockSpec((None,) + shape, lambda *_: (layer,) + (0,) * nd, pipeline_mode=pl.Buffered(1))


def _mixer_weight_specs(layer, wide_slab):
    spec = functools.partial(_layer_spec, layer)
    return [
        _layer_spec(wide_slab, (D_MODEL, IN_WIDTH)),
        spec((len(POOL_WINDOWS), POOL_GROUP_DIM, POOL_GROUP_DIM)),
        spec((1, POOL_WIDTH)),
        spec((1, GMLP_WIDTH)),
        spec((1, GMLP_WIDTH)),
        spec((GMLP_HEADS, CHUNK, CHUNK)),
        spec((CHUNK, GMLP_WIDTH)),
        spec((CONV_K, CONV_WIDTH)),
        _layer_spec(wide_slab, (D_MODEL, D_MODEL)),
    ]


def _mixer_prompt(layer, wide_slab, x, weights, batch, seq, next_weights):
    tm = MIX_TILE_ROWS
    nj = seq // tm
    row_spec = lambda width: pl.BlockSpec((tm, width), lambda b, j: (b * nj + j, 0))
    next_in, next_out, next_shapes = [], [], []
    for w in next_weights:
        slab_rows = w.shape[1] // (batch * nj)
        assert slab_rows % BF16_SUBLANE_TILE == 0
        next_in.append(pl.BlockSpec((None, slab_rows, w.shape[2]), lambda b, j: (layer + 1, b * nj + j, 0)))
        next_out.append(pl.BlockSpec((None, slab_rows, w.shape[2]), lambda b, j: (0, b * nj + j, 0)))
        next_shapes.append(jax.ShapeDtypeStruct((1,) + w.shape[1:], jnp.bfloat16))
    return pl.pallas_call(
        functools.partial(_mixer_kernel, False, tm),
        grid=(batch, nj),
        in_specs=[row_spec(D_MODEL)] + _mixer_weight_specs(layer, wide_slab) + next_in,
        out_specs=[
            row_spec(D_MODEL),
            pl.BlockSpec((None, POOL_HIST_PAD, POOL_WIDTH), lambda b, j: (b, 0, 0)),
            pl.BlockSpec((None, CONV_HIST_PAD, CONV_WIDTH), lambda b, j: (b, 0, 0)),
        ] + next_out,
        out_shape=[
            jax.ShapeDtypeStruct((batch * seq, D_MODEL), jnp.float32),
            jax.ShapeDtypeStruct((batch, POOL_HIST_PAD, POOL_WIDTH), jnp.float32),
            jax.ShapeDtypeStruct((batch, CONV_HIST_PAD, CONV_WIDTH), jnp.float32),
        ] + next_shapes,
        scratch_shapes=[
            pltpu.VMEM((POOL_HIST_PAD, POOL_WIDTH), jnp.float32),
            pltpu.VMEM((CONV_HIST_PAD, CONV_WIDTH), jnp.float32),
        ],
        compiler_params=pltpu.CompilerParams(
            dimension_semantics=("arbitrary", "arbitrary"), vmem_limit_bytes=V7X_VMEM_LIMIT_BYTES),
        name="mixer_prompt",
    )(x, *weights, *next_weights)


def _mixer_sample(layer, wide_slab, x, state_pool, state_conv, weights):
    m = x.shape[0]
    tm = SAMPLE_TILE_ROWS
    nseq = state_pool.shape[1] * tm // m
    rows = lambda width: pl.BlockSpec((tm, width), lambda i: (i, 0))
    seqs = lambda pad, width: pl.BlockSpec((None, nseq, pad, width), lambda i: (layer, i, 0, 0))
    return pl.pallas_call(
        functools.partial(_mixer_kernel, True, tm),
        grid=(m // tm,),
        in_specs=[rows(D_MODEL), seqs(POOL_HIST_PAD, POOL_WIDTH),
                  seqs(CONV_HIST_PAD, CONV_WIDTH)] + _mixer_weight_specs(layer, wide_slab),
        out_specs=[rows(D_MODEL), rows(POOL_WIDTH), rows(CONV_WIDTH), rows(GMLP_WIDTH)],
        out_shape=[
            jax.ShapeDtypeStruct((m, D_MODEL), jnp.float32),
            jax.ShapeDtypeStruct((m, POOL_WIDTH), jnp.float32),
            jax.ShapeDtypeStruct((m, CONV_WIDTH), jnp.float32),
            jax.ShapeDtypeStruct((m, GMLP_WIDTH), jnp.float32),
        ],
        compiler_params=pltpu.CompilerParams(
            dimension_semantics=("arbitrary",), vmem_limit_bytes=V7X_VMEM_LIMIT_BYTES),
        name="mixer_sample",
    )(x, state_pool, state_conv, *weights)


def _ffn_kernel(n_p, n_s, n_next, rp_hbm, rp_ref, rs_ref, w1_ref, w2_ref, g1_ref, b1_ref, g2_ref, b2_ref, *rest):
    next_f32 = rest[:n_next]
    yp_ref, ys_ref = rest[n_next:n_next + 2]
    next_bf16 = rest[n_next + 2:2 * n_next + 2]
    acc_even, acc_odd, hb_even, hb_odd, sem = rest[2 * n_next + 2:]
    i, f = pl.program_id(0), pl.program_id(1)
    n = n_p + n_s
    tm = acc_even.shape[0]
    piece = rp_ref.shape[0]
    rows = pl.ds(pl.multiple_of(f * piece, piece), piece)

    def seed(acc_ref, hb_ref, dst_rows, r):
        h = _layernorm(r, g1_ref[...], b1_ref[...])
        hb_ref[dst_rows, :] = h.astype(jnp.bfloat16)
        acc_ref[dst_rows, :] = ALPHA * h
        return h

    def finished_piece(acc_ref):
        return _layernorm(acc_ref[rows, :], g2_ref[...], b2_ref[...])

    def zero_after(*values):
        tiles = []
        for v in values:
            bits = lax.bitcast_convert_type(v, jnp.uint32)
            tiles += [bits[r:r + F32_SUBLANES, c:c + LANES]
                      for r in range(0, v.shape[0], F32_SUBLANES) for c in range(0, v.shape[1], LANES)]
        word = functools.reduce(lambda p, q: p | q, tiles)
        return lax.bitcast_convert_type(_zero_word(word), jnp.float32)

    def step(acc_cur, hb_cur, acc_oth, hb_oth):
        y = finished_piece(acc_oth)
        yp_ref[...] = y
        h_next = seed(acc_oth, hb_oth, rows, jnp.where(i + 1 < n_p, rp_ref[...], rs_ref[...]))
        for src, dst in zip(next_f32, next_bf16):
            dst[...] = src[...].astype(jnp.bfloat16)

        w1, w2 = w1_ref[...], w2_ref[...]
        if w1.dtype != jnp.bfloat16:
            w1, w2 = w1.astype(jnp.bfloat16), w2.astype(jnp.bfloat16)
        tf = w1.shape[1]
        a = jnp.square(jnp.maximum(_dot(hb_cur[...], w1), _tile_register(zero_after(y, h_next), tm, tf)))
        acc_cur[...] += _dot(a.astype(jnp.bfloat16), w2)

    @pl.when((i == 0) & (f == 0))
    def _():
        first = pltpu.make_async_copy(rp_hbm.at[pl.ds(0, tm), :], acc_even, sem)
        first.start()
        first.wait()
        seed(acc_even, hb_even, slice(None), acc_even[...])
        acc_odd[...] = jnp.zeros_like(acc_odd)

    @pl.when((i < n) & (i % 2 == 0))
    def _():
        step(acc_even, hb_even, acc_odd, hb_odd)

    @pl.when((i < n) & (i % 2 == 1))
    def _():
        step(acc_odd, hb_odd, acc_even, hb_even)

    @pl.when(i == n)
    def _():
        ys_ref[...] = finished_piece(acc_odd if n % 2 == 0 else acc_even)


def _ffn(layer, r_p, r_s, w1, w2, w_slab, g1, b1, g2, b2, next_weights):
    tm = FFN_TILE_ROWS
    tf = FFN_TILE_COLS_BF16 if w1.dtype == jnp.bfloat16 else FFN_TILE_COLS_F32
    n_p, n_s = r_p.shape[0] // tm, r_s.shape[0] // tm
    nf = D_FF // tf
    piece = tm // nf
    last = n_p + n_s - 1

    def piece_spec(first_tile, n_tiles, lag):
        def index(i, f):
            return (jnp.clip((i + lag - first_tile) * nf + f, 0, n_tiles * nf - 1), 0)
        return pl.BlockSpec((piece, D_MODEL), index)

    def slab_index(i, f):
        return jnp.minimum(i * nf + f, convert_steps - 1)

    assert n_s == 1
    convert_steps = 1 << (((last + 1) * nf).bit_length() - 1)
    next_in, next_out, next_shapes = [], [], []
    for w in next_weights:
        slab_rows = w.shape[1] // convert_steps
        assert slab_rows % BF16_SUBLANE_TILE == 0
        next_in.append(pl.BlockSpec((None, slab_rows, w.shape[2]), lambda i, f: (layer + 1, slab_index(i, f), 0)))
        next_out.append(pl.BlockSpec((None, slab_rows, w.shape[2]), lambda i, f: (0, slab_index(i, f), 0)))
        next_shapes.append(jax.ShapeDtypeStruct((1,) + w.shape[1:], jnp.bfloat16))

    vec = pl.BlockSpec((None, 1, D_MODEL), lambda i, f: (layer, 0, 0))
    chunk = lambda i, f: jnp.where(i > last, nf - 1, f)
    return pl.pallas_call(
        functools.partial(_ffn_kernel, n_p, n_s, len(next_weights)),
        grid=(n_p + n_s + 1, nf),
        in_specs=[
            pl.BlockSpec(memory_space=pl.ANY),
            piece_spec(0, n_p, 1),
            piece_spec(n_p, n_s, 1),
            pl.BlockSpec((None, D_MODEL, tf), lambda i, f: (w_slab, 0, chunk(i, f))),
            pl.BlockSpec((None, tf, D_MODEL), lambda i, f: (w_slab, chunk(i, f), 0)),
            vec, vec, vec, vec,
        ] + next_in,
        out_specs=[piece_spec(0, n_p, -1), piece_spec(n_p, n_s, -1)] + next_out,
        out_shape=[jax.ShapeDtypeStruct(r_p.shape, jnp.float32),
                   jax.ShapeDtypeStruct(r_s.shape, jnp.float32)] + next_shapes,
        scratch_shapes=[pltpu.VMEM((tm, D_MODEL), jnp.float32), pltpu.VMEM((tm, D_MODEL), jnp.float32),
                        pltpu.VMEM((tm, D_MODEL), jnp.bfloat16), pltpu.VMEM((tm, D_MODEL), jnp.bfloat16),
                        pltpu.SemaphoreType.DMA(())],
        compiler_params=pltpu.CompilerParams(
            dimension_semantics=("arbitrary", "arbitrary"), vmem_limit_bytes=V7X_VMEM_LIMIT_BYTES),
        name="ffn",
    )(r_p, r_p, r_s, w1, w2, g1, b1, g2, b2, *next_weights)


def kernel(x_prompt, x_sample, state_pool, state_conv, w_in, w_pool, pool_scale, ln_v_g, ln_v_b, w_s, b_s, w_conv, w_out, ln1_g, ln1_b, w_ff1, w_ff2, ln2_g, ln2_b):
    bf16 = jnp.bfloat16
    batch, seq, _ = x_prompt.shape
    nseq, dec_len, _ = x_sample.shape
    assert seq % MIX_TILE_ROWS == 0 and MIX_TILE_ROWS % CHUNK == 0
    assert CHUNK % dec_len == 0 and PAST_LEN % CHUNK == 0
    assert (nseq * dec_len) % SAMPLE_TILE_ROWS == 0 and SAMPLE_TILE_ROWS % CHUNK == 0
    assert (batch * seq) % FFN_TILE_ROWS == 0 and (nseq * dec_len) % FFN_TILE_ROWS == 0

    w_pool_b = w_pool.astype(bf16)
    row = lambda a: a.reshape(DEPTH, 1, -1)
    pool_scale, ln_v_g, ln_v_b = row(pool_scale), row(ln_v_g), row(ln_v_b)
    ln1_g, ln1_b, ln2_g, ln2_b = row(ln1_g), row(ln1_b), row(ln2_g), row(ln2_b)

    exact = lax.Precision.HIGHEST
    head_cols = (jnp.arange(GMLP_WIDTH)[None, :] // GMLP_HEAD_DIM == jnp.arange(GMLP_HEADS)[:, None]).astype(jnp.float32)
    pos_rows = (jnp.arange(CHUNK)[:, None] % dec_len == jnp.arange(dec_len)[None, :]).astype(jnp.float32)
    bs_prompt = jnp.einsum('lhi,hc->lic', b_s, head_cols, precision=exact)
    bs_sample = jnp.einsum('ik,lhk,hc->lic', pos_rows, b_s[:, :, :dec_len], head_cols, precision=exact)
    ws_sample = jnp.einsum('ik,lhkm,jm->lhij', pos_rows, w_s[:, :, :dec_len, :dec_len], pos_rows, precision=exact)

    xp = x_prompt.reshape(batch * seq, D_MODEL)
    xs = x_sample.reshape(nseq * dec_len, D_MODEL)
    pool_p, conv_p, p_new, cz_new, chunk_v = [], [], [], [], []
    w_in_b, w_out_b = w_in[:1].astype(bf16), w_out[:1].astype(bf16)
    w1, w2 = w_ff1, w_ff2
    for l in range(DEPTH):
        shared = (w_in_b, w_pool_b, pool_scale, ln_v_g, ln_v_b)
        tail = (w_conv, w_out_b)
        has_next = l + 1 < DEPTH
        rp, p16, cz8, *next_wide = _mixer_prompt(l, 0, xp, shared + (w_s, bs_prompt) + tail, batch, seq,
                                                 (w_in, w_out) if has_next else ())
        rs, p_s, cz_s, v_s = _mixer_sample(l, 0, xs, state_pool, state_conv,
                                           shared + (ws_sample, bs_sample) + tail)
        xp, xs, *next_ffn = _ffn(l, rp, rs, w1, w2, 0, ln1_g, ln1_b, ln2_g, ln2_b,
                                 (w_ff1, w_ff2) if has_next else ())
        if has_next:
            (w_in_b, w_out_b), (w1, w2) = next_wide, next_ffn
        pool_p.append(p16[:, POOL_HIST_PAD - POOL_HIST:])
        conv_p.append(cz8[:, CONV_HIST_PAD - CONV_HIST:])
        p_new.append(p_s.reshape(nseq, dec_len, POOL_WIDTH))
        cz_new.append(cz_s.reshape(nseq, dec_len, CONV_WIDTH))
        chunk_v.append(v_s.reshape(nseq, dec_len, GMLP_WIDTH))

    pool_s = jnp.concatenate([state_pool, jnp.stack(p_new)], axis=2)[:, :, -POOL_HIST:]
    conv_s = jnp.concatenate([state_conv, jnp.stack(cz_new)], axis=2)[:, :, -CONV_HIST:]
    return (xp.reshape(batch, seq, D_MODEL), xs.reshape(nseq, dec_len, D_MODEL),
            jnp.stack(pool_p), jnp.stack(conv_p), pool_s, conv_s, jnp.stack(chunk_v))
```

```python
import functools

import jax
import jax.numpy as jnp
from jax import lax
from jax.experimental import pallas as pl
from jax.experimental.pallas import tpu as pltpu

D_MODEL = 2048
DEPTH = 4
POOL_WIDTH = D_MODEL // 4
POOL_WINDOWS = (2, 4, 8, 16)
POOL_GROUP_DIM = POOL_WIDTH // len(POOL_WINDOWS)
POOL_HIST = max(POOL_WINDOWS) - 1
GMLP_WIDTH = D_MODEL // 2
GMLP_HEADS = 8
GMLP_HEAD_DIM = GMLP_WIDTH // GMLP_HEADS
CHUNK = 128
CONV_WIDTH = D_MODEL // 4
CONV_K = 3
CONV_HIST = CONV_K - 1
D_FF = 4 * D_MODEL
PAST_LEN = 16384
ALPHA = (2 * DEPTH) ** 0.25
LN_EPS = 1e-5

OFF_P = 0
OFF_U = OFF_P + POOL_WIDTH
OFF_V = OFF_U + GMLP_WIDTH
OFF_GB = OFF_V + GMLP_WIDTH
OFF_GC = OFF_GB + CONV_WIDTH
OFF_Z = OFF_GC + CONV_WIDTH
IN_WIDTH = OFF_Z + CONV_WIDTH

POOL_HIST_PAD = 16
CONV_HIST_PAD = 8

V7X_VMEM_LIMIT_BYTES = 60 * 1024 * 1024

MIX_TILE_ROWS = 512
SAMPLE_TILE_ROWS = 256
FFN_TILE_ROWS = 1024
FFN_TILE_COLS_F32 = 512
FFN_TILE_COLS_BF16 = 1024
BF16_SUBLANE_TILE = 16
F32_SUBLANES = 8
LANES = 128
HALF_WORD_BITS = 16


def _dot(a, b):
    return jnp.dot(a, b, preferred_element_type=jnp.float32)


def _zero_word(bits):
    return (bits >> HALF_WORD_BITS) >> HALF_WORD_BITS


def _tile_register(tile, rows, cols):
    wide = jnp.concatenate([tile] * (cols // LANES), axis=1)
    return jnp.broadcast_to(wide[None], (rows // F32_SUBLANES, F32_SUBLANES, cols)).reshape(rows, cols)


def _layernorm(x, g, b):
    mu = jnp.mean(x, axis=-1, keepdims=True)
    xc = x - mu
    var = jnp.mean(xc * xc, axis=-1, keepdims=True)
    return xc * lax.rsqrt(var + LN_EPS) * g + b


def _mixer_kernel(is_sample, tm, *refs):
    if is_sample:
        (x_ref, hp_ref, hc_ref, w_in_ref, w_pool_ref, pscale_ref, lnv_g_ref, lnv_b_ref, ws_ref, bs_ref,
         w_conv_ref, w_out_ref,
         r_ref, p_out_ref, cz_out_ref, v_out_ref) = refs
        nseq = hp_ref.shape[0]
        seq_len = tm // nseq

        def carried_history(ref, state_rows):
            _, pad_rows, width = ref.shape
            flat = ref[...].reshape(nseq * pad_rows, width)
            return pltpu.roll(flat, pad_rows - state_rows, axis=0).reshape(nseq, pad_rows, width)
    else:
        n_next = (len(refs) - 15) // 2
        (x_ref, w_in_ref, w_pool_ref, pscale_ref, lnv_g_ref, lnv_b_ref, ws_ref, bs_ref,
         w_conv_ref, w_out_ref) = refs[:10]
        next_f32 = refs[10:10 + n_next]
        r_ref, p_out_ref, cz_out_ref = refs[10 + n_next:13 + n_next]
        next_bf16 = refs[13 + n_next:13 + 2 * n_next]
        hp_scr, hc_scr = refs[13 + 2 * n_next:]
        for src, dst in zip(next_f32, next_bf16):
            dst[...] = src[...].astype(jnp.bfloat16)
        j = pl.program_id(1)

        @pl.when(j == 0)
        def _():
            hp_scr[...] = jnp.zeros_like(hp_scr)
            hc_scr[...] = jnp.zeros_like(hc_scr)

    x = x_ref[...]
    xb = x.astype(jnp.bfloat16)

    def proj(off, width):
        return _dot(xb, w_in_ref[:, off:off + width])

    def with_history(hist, new, hist_rows):
        width = new.shape[-1]
        if is_sample:
            ext = jnp.concatenate([hist, new.reshape(nseq, seq_len, width)], axis=1)
            ext = ext.reshape(nseq * (hist_rows + seq_len), width)

            def take_new(s):
                w = s.shape[-1]
                return s.reshape(nseq, hist_rows + seq_len, w)[:, hist_rows:, :].reshape(tm, w)
        else:
            ext = jnp.concatenate([hist, new], axis=0)

            def take_new(s):
                return s[hist_rows:, :]
        return ext, take_new

    v_pre = proj(OFF_V, GMLP_WIDTH)
    p = proj(OFF_P, POOL_WIDTH)
    u_pre = proj(OFF_U, GMLP_WIDTH)
    gate_c = proj(OFF_GC, CONV_WIDTH)
    z = proj(OFF_Z, CONV_WIDTH)
    gate_b = proj(OFF_GB, CONV_WIDTH)

    v = _layernorm(jax.nn.gelu(v_pre), lnv_g_ref[...], lnv_b_ref[...])
    if is_sample:
        v_out_ref[...] = v
    vb = v.astype(jnp.bfloat16)
    row = lax.broadcasted_iota(jnp.int32, (CHUNK, CHUNK), 0)
    col = lax.broadcasted_iota(jnp.int32, (CHUNK, CHUNK), 1)
    if is_sample:
        mask = (row // seq_len == col // seq_len) & (col <= row)
    else:
        mask = col <= row
    nchunk = tm // CHUNK
    mixed_cols = []
    for hd in range(GMLP_HEADS):
        wm = jnp.where(mask, ws_ref[hd], 0.0).astype(jnp.bfloat16)
        lo = hd * GMLP_HEAD_DIM
        rhs = jnp.concatenate(
            [vb[c * CHUNK:(c + 1) * CHUNK, lo:lo + GMLP_HEAD_DIM] for c in range(nchunk)], axis=1)
        out = _dot(wm, rhs)
        mixed_cols.append(jnp.concatenate(
            [out[:, c * GMLP_HEAD_DIM:(c + 1) * GMLP_HEAD_DIM] for c in range(nchunk)], axis=0))
    mixed = jnp.concatenate(mixed_cols, axis=1)
    bias = jnp.concatenate([bs_ref[...]] * nchunk, axis=0)
    b_out = jax.nn.gelu(u_pre) * (mixed + bias)

    hist_p = carried_history(hp_ref, POOL_HIST) if is_sample else hp_scr[...]
    ext, take_new = with_history(hist_p, p, POOL_HIST_PAD)
    if is_sample:
        pos1 = None
    else:
        pos1 = (j * tm + 1 + lax.broadcasted_iota(jnp.int32, (tm, 1), 0)).astype(jnp.float32)
    s = ext
    a_parts = []
    for g, w in enumerate(POOL_WINDOWS):
        s = s + pltpu.roll(s, w // 2, axis=0)
        lo = g * POOL_GROUP_DIM
        win = take_new(s[:, :POOL_GROUP_DIM])
        if is_sample:
            mean = win * (1.0 / w)
        else:
            mean = win / jnp.minimum(jnp.float32(w), pos1)
        d = mean - p[:, lo:lo + POOL_GROUP_DIM]
        a_parts.append(_dot(d.astype(jnp.bfloat16), w_pool_ref[g]))
        if g + 1 < len(POOL_WINDOWS):
            s = s[:, POOL_GROUP_DIM:]
    a_out = jnp.concatenate(a_parts, axis=1) * pscale_ref[...]

    cz = gate_c * z
    hist_c = carried_history(hc_ref, CONV_HIST) if is_sample else hc_scr[...]
    ext_c, take_new_c = with_history(hist_c, cz, CONV_HIST_PAD)
    y = w_conv_ref[CONV_K - 1:CONV_K, :] * ext_c
    for k in range(CONV_K - 1):
        y = y + w_conv_ref[k:k + 1, :] * pltpu.roll(ext_c, CONV_K - 1 - k, axis=0)
    c_out = gate_b * take_new_c(y)

    o = _dot(c_out.astype(jnp.bfloat16), w_out_ref[POOL_WIDTH + GMLP_WIDTH:, :])
    o = o + _dot(a_out.astype(jnp.bfloat16), w_out_ref[:POOL_WIDTH, :])
    o = o + _dot(b_out.astype(jnp.bfloat16), w_out_ref[POOL_WIDTH:POOL_WIDTH + GMLP_WIDTH, :])
    r_ref[...] = ALPHA * x + o

    if is_sample:
        rows_p = jnp.concatenate([hist_p, p.reshape(nseq, seq_len, POOL_WIDTH)], axis=1)
        p_out_ref[...] = rows_p[:, seq_len:, :]
        cz_out_ref[...] = cz
    else:
        hp_new = p[tm - POOL_HIST_PAD:, :]
        hc_new = cz[tm - CONV_HIST_PAD:, :]
        hp_scr[...] = hp_new
        hc_scr[...] = hc_new
        p_out_ref[...] = hp_new
        cz_out_ref[...] = hc_new


def _layer_spec(layer, shape):
    nd = len(shape)
    return pl.BlockSpec((None,) + shape, lambda *_: (layer,) + (0,) * nd, pipeline_mode=pl.Buffered(1))


def _mixer_weight_specs(layer, wide_slab):
    spec = functools.partial(_layer_spec, layer)
    return [
        _layer_spec(wide_slab, (D_MODEL, IN_WIDTH)),
        spec((len(POOL_WINDOWS), POOL_GROUP_DIM, POOL_GROUP_DIM)),
        spec((1, POOL_WIDTH)),
        spec((1, GMLP_WIDTH)),
        spec((1, GMLP_WIDTH)),
        spec((GMLP_HEADS, CHUNK, CHUNK)),
        spec((CHUNK, GMLP_WIDTH)),
        spec((CONV_K, CONV_WIDTH)),
        _layer_spec(wide_slab, (D_MODEL, D_MODEL)),
    ]


def _mixer_prompt(layer, wide_slab, x, weights, batch, seq, next_weights):
    tm = MIX_TILE_ROWS
    nj = seq // tm
    row_spec = lambda width: pl.BlockSpec((tm, width), lambda b, j: (b * nj + j, 0))
    next_in, next_out, next_shapes = [], [], []
    for w in next_weights:
        slab_rows = w.shape[1] // (batch * nj)
        assert slab_rows % BF16_SUBLANE_TILE == 0
        next_in.append(pl.BlockSpec((None, slab_rows, w.shape[2]), lambda b, j: (layer + 1, b * nj + j, 0)))
        next_out.append(pl.BlockSpec((None, slab_rows, w.shape[2]), lambda b, j: (0, b * nj + j, 0)))
        next_shapes.append(jax.ShapeDtypeStruct((1,) + w.shape[1:], jnp.bfloat16))
    return pl.pallas_call(
        functools.partial(_mixer_kernel, False, tm),
        grid=(batch, nj),
        in_specs=[row_spec(D_MODEL)] + _mixer_weight_specs(layer, wide_slab) + next_in,
        out_specs=[
            row_spec(D_MODEL),
            pl.BlockSpec((None, POOL_HIST_PAD, POOL_WIDTH), lambda b, j: (b, 0, 0)),
            pl.BlockSpec((None, CONV_HIST_PAD, CONV_WIDTH), lambda b, j: (b, 0, 0)),
        ] + next_out,
        out_shape=[
            jax.ShapeDtypeStruct((batch * seq, D_MODEL), jnp.float32),
            jax.ShapeDtypeStruct((batch, POOL_HIST_PAD, POOL_WIDTH), jnp.float32),
            jax.ShapeDtypeStruct((batch, CONV_HIST_PAD, CONV_WIDTH), jnp.float32),
        ] + next_shapes,
        scratch_shapes=[
            pltpu.VMEM((POOL_HIST_PAD, POOL_WIDTH), jnp.float32),
            pltpu.VMEM((CONV_HIST_PAD, CONV_WIDTH), jnp.float32),
        ],
        compiler_params=pltpu.CompilerParams(
            dimension_semantics=("arbitrary", "arbitrary"), vmem_limit_bytes=V7X_VMEM_LIMIT_BYTES),
        name="mixer_prompt",
    )(x, *weights, *next_weights)


def _mixer_sample(layer, wide_slab, x, state_pool, state_conv, weights):
    m = x.shape[0]
    tm = SAMPLE_TILE_ROWS
    nseq = state_pool.shape[1] * tm // m
    rows = lambda width: pl.BlockSpec((tm, width), lambda i: (i, 0))
    seqs = lambda pad, width: pl.BlockSpec((None, nseq, pad, width), lambda i: (layer, i, 0, 0))
    return pl.pallas_call(
        functools.partial(_mixer_kernel, True, tm),
        grid=(m // tm,),
        in_specs=[rows(D_MODEL), seqs(POOL_HIST_PAD, POOL_WIDTH),
                  seqs(CONV_HIST_PAD, CONV_WIDTH)] + _mixer_weight_specs(layer, wide_slab),
        out_specs=[rows(D_MODEL),
                   pl.BlockSpec((nseq, POOL_HIST_PAD, POOL_WIDTH), lambda i: (i, 0, 0)),
                   rows(CONV_WIDTH), rows(GMLP_WIDTH)],
        out_shape=[
            jax.ShapeDtypeStruct((m, D_MODEL), jnp.float32),
            jax.ShapeDtypeStruct((state_pool.shape[1], POOL_HIST_PAD, POOL_WIDTH), jnp.float32),
            jax.ShapeDtypeStruct((m, CONV_WIDTH), jnp.float32),
            jax.ShapeDtypeStruct((m, GMLP_WIDTH), jnp.float32),
        ],
        compiler_params=pltpu.CompilerParams(
            dimension_semantics=("arbitrary",), vmem_limit_bytes=V7X_VMEM_LIMIT_BYTES),
        name="mixer_sample",
    )(x, state_pool, state_conv, *weights)


def _ffn_kernel(n_p, n_s, n_next, rp_hbm, rp_ref, rs_ref, w1_ref, w2_ref, g1_ref, b1_ref, g2_ref, b2_ref, *rest):
    next_f32 = rest[:n_next]
    yp_ref, ys_ref = rest[n_next:n_next + 2]
    next_bf16 = rest[n_next + 2:2 * n_next + 2]
    acc_even, acc_odd, hb_even, hb_odd, sem = rest[2 * n_next + 2:]
    i, f = pl.program_id(0), pl.program_id(1)
    n = n_p + n_s
    tm = acc_even.shape[0]
    piece = rp_ref.shape[0]
    rows = pl.ds(pl.multiple_of(f * piece, piece), piece)

    def seed(acc_ref, hb_ref, dst_rows, r):
        h = _layernorm(r, g1_ref[...], b1_ref[...])
        hb_ref[dst_rows, :] = h.astype(jnp.bfloat16)
        acc_ref[dst_rows, :] = ALPHA * h
        return h

    def finished_piece(acc_ref):
        return _layernorm(acc_ref[rows, :], g2_ref[...], b2_ref[...])

    def zero_after(*values):
        tiles = []
        for v in values:
            bits = lax.bitcast_convert_type(v, jnp.uint32)
            tiles += [bits[r:r + F32_SUBLANES, c:c + LANES]
                      for r in range(0, v.shape[0], F32_SUBLANES) for c in range(0, v.shape[1], LANES)]
        word = functools.reduce(lambda p, q: p | q, tiles)
        return lax.bitcast_convert_type(_zero_word(word), jnp.float32)

    def step(acc_cur, hb_cur, acc_oth, hb_oth):
        y = finished_piece(acc_oth)
        yp_ref[...] = y
        h_next = seed(acc_oth, hb_oth, rows, jnp.where(i + 1 < n_p, rp_ref[...], rs_ref[...]))
        for src, dst in zip(next_f32, next_bf16):
            dst[...] = src[...].astype(jnp.bfloat16)

        w1, w2 = w1_ref[...], w2_ref[...]
        if w1.dtype != jnp.bfloat16:
            w1, w2 = w1.astype(jnp.bfloat16), w2.astype(jnp.bfloat16)
        tf = w1.shape[1]
        a = jnp.square(jnp.maximum(_dot(hb_cur[...], w1), _tile_register(zero_after(y, h_next), tm, tf)))
        acc_cur[...] += _dot(a.astype(jnp.bfloat16), w2)

    @pl.when((i == 0) & (f == 0))
    def _():
        first = pltpu.make_async_copy(rp_hbm.at[pl.ds(0, tm), :], acc_even, sem)
        first.start()
        first.wait()
        seed(acc_even, hb_even, slice(None), acc_even[...])
        acc_odd[...] = jnp.zeros_like(acc_odd)

    @pl.when((i < n) & (i % 2 == 0))
    def _():
        step(acc_even, hb_even, acc_odd, hb_odd)

    @pl.when((i < n) & (i % 2 == 1))
    def _():
        step(acc_odd, hb_odd, acc_even, hb_even)

    @pl.when(i == n)
    def _():
        ys_ref[...] = finished_piece(acc_odd if n % 2 == 0 else acc_even)


def _ffn(layer, r_p, r_s, w1, w2, w_slab, g1, b1, g2, b2, next_weights):
    tm = FFN_TILE_ROWS
    tf = FFN_TILE_COLS_BF16 if w1.dtype == jnp.bfloat16 else FFN_TILE_COLS_F32
    n_p, n_s = r_p.shape[0] // tm, r_s.shape[0] // tm
    nf = D_FF // tf
    piece = tm // nf
    last = n_p + n_s - 1

    def piece_spec(first_tile, n_tiles, lag):
        def index(i, f):
            return (jnp.clip((i + lag - first_tile) * nf + f, 0, n_tiles * nf - 1), 0)
        return pl.BlockSpec((piece, D_MODEL), index)

    def slab_index(i, f):
        return jnp.minimum(i * nf + f, convert_steps - 1)

    assert n_s == 1
    convert_steps = 1 << (((last + 1) * nf).bit_length() - 1)
    next_in, next_out, next_shapes = [], [], []
    for w in next_weights:
        slab_rows = w.shape[1] // convert_steps
        assert slab_rows % BF16_SUBLANE_TILE == 0
        next_in.append(pl.BlockSpec((None, slab_rows, w.shape[2]), lambda i, f: (layer + 1, slab_index(i, f), 0)))
        next_out.append(pl.BlockSpec((None, slab_rows, w.shape[2]), lambda i, f: (0, slab_index(i, f), 0)))
        next_shapes.append(jax.ShapeDtypeStruct((1,) + w.shape[1:], jnp.bfloat16))

    vec = pl.BlockSpec((None, 1, D_MODEL), lambda i, f: (layer, 0, 0))
    chunk = lambda i, f: jnp.where(i > last, nf - 1, f)
    return pl.pallas_call(
        functools.partial(_ffn_kernel, n_p, n_s, len(next_weights)),
        grid=(n_p + n_s + 1, nf),
        in_specs=[
            pl.BlockSpec(memory_space=pl.ANY),
            piece_spec(0, n_p, 1),
            piece_spec(n_p, n_s, 1),
            pl.BlockSpec((None, D_MODEL, tf), lambda i, f: (w_slab, 0, chunk(i, f))),
            pl.BlockSpec((None, tf, D_MODEL), lambda i, f: (w_slab, chunk(i, f), 0)),
            vec, vec, vec, vec,
        ] + next_in,
        out_specs=[piece_spec(0, n_p, -1), piece_spec(n_p, n_s, -1)] + next_out,
        out_shape=[jax.ShapeDtypeStruct(r_p.shape, jnp.float32),
                   jax.ShapeDtypeStruct(r_s.shape, jnp.float32)] + next_shapes,
        scratch_shapes=[pltpu.VMEM((tm, D_MODEL), jnp.float32), pltpu.VMEM((tm, D_MODEL), jnp.float32),
                        pltpu.VMEM((tm, D_MODEL), jnp.bfloat16), pltpu.VMEM((tm, D_MODEL), jnp.bfloat16),
                        pltpu.SemaphoreType.DMA(())],
        compiler_params=pltpu.CompilerParams(
            dimension_semantics=("arbitrary", "arbitrary"), vmem_limit_bytes=V7X_VMEM_LIMIT_BYTES),
        name="ffn",
    )(r_p, r_p, r_s, w1, w2, g1, b1, g2, b2, *next_weights)


def kernel(x_prompt, x_sample, state_pool, state_conv, w_in, w_pool, pool_scale, ln_v_g, ln_v_b, w_s, b_s, w_conv, w_out, ln1_g, ln1_b, w_ff1, w_ff2, ln2_g, ln2_b):
    bf16 = jnp.bfloat16
    batch, seq, _ = x_prompt.shape
    nseq, dec_len, _ = x_sample.shape
    assert seq % MIX_TILE_ROWS == 0 and MIX_TILE_ROWS % CHUNK == 0
    assert CHUNK % dec_len == 0 and PAST_LEN % CHUNK == 0
    assert (nseq * dec_len) % SAMPLE_TILE_ROWS == 0 and SAMPLE_TILE_ROWS % CHUNK == 0
    assert (batch * seq) % FFN_TILE_ROWS == 0 and (nseq * dec_len) % FFN_TILE_ROWS == 0

    w_pool_b = w_pool.astype(bf16)
    row = lambda a: a.reshape(DEPTH, 1, -1)
    pool_scale, ln_v_g, ln_v_b = row(pool_scale), row(ln_v_g), row(ln_v_b)
    ln1_g, ln1_b, ln2_g, ln2_b = row(ln1_g), row(ln1_b), row(ln2_g), row(ln2_b)

    exact = lax.Precision.HIGHEST
    head_cols = (jnp.arange(GMLP_WIDTH)[None, :] // GMLP_HEAD_DIM == jnp.arange(GMLP_HEADS)[:, None]).astype(jnp.float32)
    pos_rows = (jnp.arange(CHUNK)[:, None] % dec_len == jnp.arange(dec_len)[None, :]).astype(jnp.float32)
    bs_prompt = jnp.einsum('lhi,hc->lic', b_s, head_cols, precision=exact)
    bs_sample = jnp.einsum('ik,lhk,hc->lic', pos_rows, b_s[:, :, :dec_len], head_cols, precision=exact)
    ws_sample = jnp.einsum('ik,lhkm,jm->lhij', pos_rows, w_s[:, :, :dec_len, :dec_len], pos_rows, precision=exact)

    xp = x_prompt.reshape(batch * seq, D_MODEL)
    xs = x_sample.reshape(nseq * dec_len, D_MODEL)
    pool_p, conv_p, p_new, cz_new, chunk_v = [], [], [], [], []
    w_in_b, w_out_b = w_in[:1].astype(bf16), w_out[:1].astype(bf16)
    w1, w2 = w_ff1, w_ff2
    for l in range(DEPTH):
        shared = (w_in_b, w_pool_b, pool_scale, ln_v_g, ln_v_b)
        tail = (w_conv, w_out_b)
        has_next = l + 1 < DEPTH
        rp, p16, cz8, *next_wide = _mixer_prompt(l, 0, xp, shared + (w_s, bs_prompt) + tail, batch, seq,
                                                 (w_in, w_out) if has_next else ())
        rs, p_s, cz_s, v_s = _mixer_sample(l, 0, xs, state_pool, state_conv,
                                           shared + (ws_sample, bs_sample) + tail)
        xp, xs, *next_ffn = _ffn(l, rp, rs, w1, w2, 0, ln1_g, ln1_b, ln2_g, ln2_b,
                                 (w_ff1, w_ff2) if has_next else ())
        if has_next:
            (w_in_b, w_out_b), (w1, w2) = next_wide, next_ffn
        pool_p.append(p16[:, POOL_HIST_PAD - POOL_HIST:])
        conv_p.append(cz8[:, CONV_HIST_PAD - CONV_HIST:])
        p_new.append(p_s)
        cz_new.append(cz_s.reshape(nseq, dec_len, CONV_WIDTH))
        chunk_v.append(v_s.reshape(nseq, dec_len, GMLP_WIDTH))

    pool_s = jnp.stack(p_new)[:, :, -POOL_HIST:]
    conv_s = jnp.concatenate([state_conv, jnp.stack(cz_new)], axis=2)[:, :, -CONV_HIST:]
    return (xp.reshape(batch, seq, D_MODEL), xs.reshape(nseq, dec_len, D_MODEL),
            jnp.stack(pool_p), jnp.stack(conv_p), pool_s, conv_s, jnp.stack(chunk_v))
```

```python
import functools

import jax
import jax.numpy as jnp
from jax import lax
from jax.experimental import pallas as pl
from jax.experimental.pallas import tpu as pltpu

D_MODEL = 2048
DEPTH = 4
POOL_WIDTH = D_MODEL // 4
POOL_WINDOWS = (2, 4, 8, 16)
POOL_GROUP_DIM = POOL_WIDTH // len(POOL_WINDOWS)
POOL_HIST = max(POOL_WINDOWS) - 1
GMLP_WIDTH = D_MODEL // 2
GMLP_HEADS = 8
GMLP_HEAD_DIM = GMLP_WIDTH // GMLP_HEADS
CHUNK = 128
CONV_WIDTH = D_MODEL // 4
CONV_K = 3
CONV_HIST = CONV_K - 1
D_FF = 4 * D_MODEL
PAST_LEN = 16384
ALPHA = (2 * DEPTH) ** 0.25
LN_EPS = 1e-5

OFF_P = 0
OFF_U = OFF_P + POOL_WIDTH
OFF_V = OFF_U + GMLP_WIDTH
OFF_GB = OFF_V + GMLP_WIDTH
OFF_GC = OFF_GB + CONV_WIDTH
OFF_Z = OFF_GC + CONV_WIDTH
IN_WIDTH = OFF_Z + CONV_WIDTH

POOL_HIST_PAD = 16
CONV_HIST_PAD = 8

V7X_VMEM_LIMIT_BYTES = 60 * 1024 * 1024

MIX_TILE_ROWS = 512
SAMPLE_TILE_ROWS = 256
FFN_TILE_ROWS = 1024
FFN_TILE_COLS_F32 = 512
FFN_TILE_COLS_BF16 = 1024
FIRST_TILE_CHUNKS = 4
BF16_SUBLANE_TILE = 16
F32_SUBLANES = 8
LANES = 128
HALF_WORD_BITS = 16


def _dot(a, b):
    return jnp.dot(a, b, preferred_element_type=jnp.float32)


def _zero_word(bits):
    return (bits >> HALF_WORD_BITS) >> HALF_WORD_BITS


def _tile_register(tile, rows, cols):
    wide = jnp.concatenate([tile] * (cols // LANES), axis=1)
    return jnp.broadcast_to(wide[None], (rows // F32_SUBLANES, F32_SUBLANES, cols)).reshape(rows, cols)


def _layernorm(x, g, b):
    mu = jnp.mean(x, axis=-1, keepdims=True)
    xc = x - mu
    var = jnp.mean(xc * xc, axis=-1, keepdims=True)
    return xc * lax.rsqrt(var + LN_EPS) * g + b


def _mixer_kernel(is_sample, tm, *refs):
    if is_sample:
        (x_ref, hp_ref, hc_ref, w_in_ref, w_pool_ref, pscale_ref, lnv_g_ref, lnv_b_ref, ws_ref, bs_ref,
         w_conv_ref, w_out_ref,
         r_ref, p_out_ref, cz_out_ref, v_out_ref) = refs
        nseq = hp_ref.shape[0]
        seq_len = tm // nseq

        def carried_history(ref, state_rows):
            _, pad_rows, width = ref.shape
            flat = ref[...].reshape(nseq * pad_rows, width)
            return pltpu.roll(flat, pad_rows - state_rows, axis=0).reshape(nseq, pad_rows, width)
    else:
        n_next = (len(refs) - 15) // 2
        (x_ref, w_in_ref, w_pool_ref, pscale_ref, lnv_g_ref, lnv_b_ref, ws_ref, bs_ref,
         w_conv_ref, w_out_ref) = refs[:10]
        next_f32 = refs[10:10 + n_next]
        r_ref, p_out_ref, cz_out_ref = refs[10 + n_next:13 + n_next]
        next_bf16 = refs[13 + n_next:13 + 2 * n_next]
        hp_scr, hc_scr = refs[13 + 2 * n_next:]
        for src, dst in zip(next_f32, next_bf16):
            dst[...] = src[...].astype(jnp.bfloat16)
        j = pl.program_id(1)

        @pl.when(j == 0)
        def _():
            hp_scr[...] = jnp.zeros_like(hp_scr)
            hc_scr[...] = jnp.zeros_like(hc_scr)

    x = x_ref[...]
    xb = x.astype(jnp.bfloat16)

    def proj(off, width):
        return _dot(xb, w_in_ref[:, off:off + width])

    def with_history(hist, new, hist_rows):
        width = new.shape[-1]
        if is_sample:
            ext = jnp.concatenate([hist, new.reshape(nseq, seq_len, width)], axis=1)
            ext = ext.reshape(nseq * (hist_rows + seq_len), width)

            def take_new(s):
                w = s.shape[-1]
                return s.reshape(nseq, hist_rows + seq_len, w)[:, hist_rows:, :].reshape(tm, w)
        else:
            ext = jnp.concatenate([hist, new], axis=0)

            def take_new(s):
                return s[hist_rows:, :]
        return ext, take_new

    v_pre = proj(OFF_V, GMLP_WIDTH)
    p = proj(OFF_P, POOL_WIDTH)
    u_pre = proj(OFF_U, GMLP_WIDTH)
    gate_c = proj(OFF_GC, CONV_WIDTH)
    z = proj(OFF_Z, CONV_WIDTH)
    gate_b = proj(OFF_GB, CONV_WIDTH)

    v = _layernorm(jax.nn.gelu(v_pre), lnv_g_ref[...], lnv_b_ref[...])
    if is_sample:
        v_out_ref[...] = v
    vb = v.astype(jnp.bfloat16)
    row = lax.broadcasted_iota(jnp.int32, (CHUNK, CHUNK), 0)
    col = lax.broadcasted_iota(jnp.int32, (CHUNK, CHUNK), 1)
    if is_sample:
        mask = (row // seq_len == col // seq_len) & (col <= row)
    else:
        mask = col <= row
    nchunk = tm // CHUNK
    mixed_cols = []
    for hd in range(GMLP_HEADS):
        wm = jnp.where(mask, ws_ref[hd], 0.0).astype(jnp.bfloat16)
        lo = hd * GMLP_HEAD_DIM
        rhs = jnp.concatenate(
            [vb[c * CHUNK:(c + 1) * CHUNK, lo:lo + GMLP_HEAD_DIM] for c in range(nchunk)], axis=1)
        out = _dot(wm, rhs)
        mixed_cols.append(jnp.concatenate(
            [out[:, c * GMLP_HEAD_DIM:(c + 1) * GMLP_HEAD_DIM] for c in range(nchunk)], axis=0))
    mixed = jnp.concatenate(mixed_cols, axis=1)
    bias = jnp.concatenate([bs_ref[...]] * nchunk, axis=0)
    b_out = jax.nn.gelu(u_pre) * (mixed + bias)

    hist_p = carried_history(hp_ref, POOL_HIST) if is_sample else hp_scr[...]
    ext, take_new = with_history(hist_p, p, POOL_HIST_PAD)
    if is_sample:
        pos1 = None
    else:
        pos1 = (j * tm + 1 + lax.broadcasted_iota(jnp.int32, (tm, 1), 0)).astype(jnp.float32)
    s = ext
    a_parts = []
    for g, w in enumerate(POOL_WINDOWS):
        s = s + pltpu.roll(s, w // 2, axis=0)
        lo = g * POOL_GROUP_DIM
        win = take_new(s[:, :POOL_GROUP_DIM])
        if is_sample:
            mean = win * (1.0 / w)
        else:
            mean = win / jnp.minimum(jnp.float32(w), pos1)
        d = mean - p[:, lo:lo + POOL_GROUP_DIM]
        a_parts.append(_dot(d.astype(jnp.bfloat16), w_pool_ref[g]))
        if g + 1 < len(POOL_WINDOWS):
            s = s[:, POOL_GROUP_DIM:]
    a_out = jnp.concatenate(a_parts, axis=1) * pscale_ref[...]

    cz = gate_c * z
    hist_c = carried_history(hc_ref, CONV_HIST) if is_sample else hc_scr[...]
    ext_c, take_new_c = with_history(hist_c, cz, CONV_HIST_PAD)
    y = w_conv_ref[CONV_K - 1:CONV_K, :] * ext_c
    for k in range(CONV_K - 1):
        y = y + w_conv_ref[k:k + 1, :] * pltpu.roll(ext_c, CONV_K - 1 - k, axis=0)
    c_out = gate_b * take_new_c(y)

    o = _dot(c_out.astype(jnp.bfloat16), w_out_ref[POOL_WIDTH + GMLP_WIDTH:, :])
    o = o + _dot(a_out.astype(jnp.bfloat16), w_out_ref[:POOL_WIDTH, :])
    o = o + _dot(b_out.astype(jnp.bfloat16), w_out_ref[POOL_WIDTH:POOL_WIDTH + GMLP_WIDTH, :])
    r_ref[...] = ALPHA * x + o

    if is_sample:
        p_out_ref[...] = p
        cz_out_ref[...] = cz
    else:
        hp_new = p[tm - POOL_HIST_PAD:, :]
        hc_new = cz[tm - CONV_HIST_PAD:, :]
        hp_scr[...] = hp_new
        hc_scr[...] = hc_new
        p_out_ref[...] = hp_new
        cz_out_ref[...] = hc_new


def _layer_spec(layer, shape):
    nd = len(shape)
    return pl.BlockSpec((None,) + shape, lambda *_: (layer,) + (0,) * nd, pipeline_mode=pl.Buffered(1))


def _mixer_weight_specs(layer, wide_slab):
    spec = functools.partial(_layer_spec, layer)
    return [
        _layer_spec(wide_slab, (D_MODEL, IN_WIDTH)),
        spec((len(POOL_WINDOWS), POOL_GROUP_DIM, POOL_GROUP_DIM)),
        spec((1, POOL_WIDTH)),
        spec((1, GMLP_WIDTH)),
        spec((1, GMLP_WIDTH)),
        spec((GMLP_HEADS, CHUNK, CHUNK)),
        spec((CHUNK, GMLP_WIDTH)),
        spec((CONV_K, CONV_WIDTH)),
        _layer_spec(wide_slab, (D_MODEL, D_MODEL)),
    ]


def _mixer_prompt(layer, wide_slab, x, weights, batch, seq, next_weights):
    tm = MIX_TILE_ROWS
    nj = seq // tm
    row_spec = lambda width: pl.BlockSpec((tm, width), lambda b, j: (b * nj + j, 0))
    next_in, next_out, next_shapes = [], [], []
    for w in next_weights:
        slab_rows = w.shape[1] // (batch * nj)
        assert slab_rows % BF16_SUBLANE_TILE == 0
        next_in.append(pl.BlockSpec((None, slab_rows, w.shape[2]), lambda b, j: (layer + 1, b * nj + j, 0)))
        next_out.append(pl.BlockSpec((None, slab_rows, w.shape[2]), lambda b, j: (0, b * nj + j, 0)))
        next_shapes.append(jax.ShapeDtypeStruct((1,) + w.shape[1:], jnp.bfloat16))
    return pl.pallas_call(
        functools.partial(_mixer_kernel, False, tm),
        grid=(batch, nj),
        in_specs=[row_spec(D_MODEL)] + _mixer_weight_specs(layer, wide_slab) + next_in,
        out_specs=[
            row_spec(D_MODEL),
            pl.BlockSpec((None, POOL_HIST_PAD, POOL_WIDTH), lambda b, j: (b, 0, 0)),
            pl.BlockSpec((None, CONV_HIST_PAD, CONV_WIDTH), lambda b, j: (b, 0, 0)),
        ] + next_out,
        out_shape=[
            jax.ShapeDtypeStruct((batch * seq, D_MODEL), jnp.float32),
            jax.ShapeDtypeStruct((batch, POOL_HIST_PAD, POOL_WIDTH), jnp.float32),
            jax.ShapeDtypeStruct((batch, CONV_HIST_PAD, CONV_WIDTH), jnp.float32),
        ] + next_shapes,
        scratch_shapes=[
            pltpu.VMEM((POOL_HIST_PAD, POOL_WIDTH), jnp.float32),
            pltpu.VMEM((CONV_HIST_PAD, CONV_WIDTH), jnp.float32),
        ],
        compiler_params=pltpu.CompilerParams(
            dimension_semantics=("arbitrary", "arbitrary"), vmem_limit_bytes=V7X_VMEM_LIMIT_BYTES),
        name="mixer_prompt",
    )(x, *weights, *next_weights)


def _mixer_sample(layer, wide_slab, x, state_pool, state_conv, weights):
    m = x.shape[0]
    tm = SAMPLE_TILE_ROWS
    nseq = state_pool.shape[1] * tm // m
    rows = lambda width: pl.BlockSpec((tm, width), lambda i: (i, 0))
    seqs = lambda pad, width: pl.BlockSpec((None, nseq, pad, width), lambda i: (layer, i, 0, 0))
    return pl.pallas_call(
        functools.partial(_mixer_kernel, True, tm),
        grid=(m // tm,),
        in_specs=[rows(D_MODEL), seqs(POOL_HIST_PAD, POOL_WIDTH),
                  seqs(CONV_HIST_PAD, CONV_WIDTH)] + _mixer_weight_specs(layer, wide_slab),
        out_specs=[rows(D_MODEL), rows(POOL_WIDTH), rows(CONV_WIDTH), rows(GMLP_WIDTH)],
        out_shape=[
            jax.ShapeDtypeStruct((m, D_MODEL), jnp.float32),
            jax.ShapeDtypeStruct((m, POOL_WIDTH), jnp.float32),
            jax.ShapeDtypeStruct((m, CONV_WIDTH), jnp.float32),
            jax.ShapeDtypeStruct((m, GMLP_WIDTH), jnp.float32),
        ],
        compiler_params=pltpu.CompilerParams(
            dimension_semantics=("arbitrary",), vmem_limit_bytes=V7X_VMEM_LIMIT_BYTES),
        name="mixer_sample",
    )(x, state_pool, state_conv, *weights)


def _ffn_kernel(n_p, n_s, n_next, rp_hbm, rp_ref, rs_ref, w1_ref, w2_ref, g1_ref, b1_ref, g2_ref, b2_ref, *rest):
    next_f32 = rest[:n_next]
    yp_ref, ys_ref = rest[n_next:n_next + 2]
    next_bf16 = rest[n_next + 2:2 * n_next + 2]
    acc_even, acc_odd, hb_even, hb_odd, sem = rest[2 * n_next + 2:]
    i, f = pl.program_id(0), pl.program_id(1)
    n = n_p + n_s
    tm = acc_even.shape[0]
    piece = rp_ref.shape[0]
    rows = pl.ds(pl.multiple_of(f * piece, piece), piece)

    def seed(acc_ref, hb_ref, dst_rows, r):
        h = _layernorm(r, g1_ref[...], b1_ref[...])
        hb_ref[dst_rows, :] = h.astype(jnp.bfloat16)
        acc_ref[dst_rows, :] = ALPHA * h
        return h

    def finished_piece(acc_ref):
        return _layernorm(acc_ref[rows, :], g2_ref[...], b2_ref[...])

    def zero_after(*values):
        tiles = []
        for v in values:
            bits = lax.bitcast_convert_type(v, jnp.uint32)
            tiles += [bits[r:r + F32_SUBLANES, c:c + LANES]
                      for r in range(0, v.shape[0], F32_SUBLANES) for c in range(0, v.shape[1], LANES)]
        word = functools.reduce(lambda p, q: p | q, tiles)
        return lax.bitcast_convert_type(_zero_word(word), jnp.float32)

    def step(acc_cur, hb_cur, acc_oth, hb_oth):
        y = finished_piece(acc_oth)
        yp_ref[...] = y
        h_next = seed(acc_oth, hb_oth, rows, jnp.where(i + 1 < n_p, rp_ref[...], rs_ref[...]))
        for src, dst in zip(next_f32, next_bf16):
            dst[...] = src[...].astype(jnp.bfloat16)

        w1, w2 = w1_ref[...], w2_ref[...]
        if w1.dtype != jnp.bfloat16:
            w1, w2 = w1.astype(jnp.bfloat16), w2.astype(jnp.bfloat16)
        tf = w1.shape[1]
        a = jnp.square(jnp.maximum(_dot(hb_cur[...], w1), _tile_register(zero_after(y, h_next), tm, tf)))
        acc_cur[...] += _dot(a.astype(jnp.bfloat16), w2)

    @pl.when((i == 0) & (f == 0))
    def _():
        chunk_rows = tm // FIRST_TILE_CHUNKS
        chunks = [pl.ds(c * chunk_rows, chunk_rows) for c in range(FIRST_TILE_CHUNKS)]
        copies = [pltpu.make_async_copy(rp_hbm.at[rows_c, :], acc_even.at[rows_c, :], sem.at[c])
                  for c, rows_c in enumerate(chunks)]
        for copy in copies:
            copy.start()
        acc_odd[...] = jnp.zeros_like(acc_odd)
        for copy, rows_c in zip(copies, chunks):
            copy.wait()
            seed(acc_even, hb_even, rows_c, acc_even[rows_c, :])

    @pl.when((i < n) & (i % 2 == 0))
    def _():
        step(acc_even, hb_even, acc_odd, hb_odd)

    @pl.when((i < n) & (i % 2 == 1))
    def _():
        step(acc_odd, hb_odd, acc_even, hb_even)

    @pl.when(i == n)
    def _():
        ys_ref[...] = finished_piece(acc_odd if n % 2 == 0 else acc_even)


def _ffn(layer, r_p, r_s, w1, w2, w_slab, g1, b1, g2, b2, next_weights):
    tm = FFN_TILE_ROWS
    tf = FFN_TILE_COLS_BF16 if w1.dtype == jnp.bfloat16 else FFN_TILE_COLS_F32
    n_p, n_s = r_p.shape[0] // tm, r_s.shape[0] // tm
    nf = D_FF // tf
    piece = tm // nf
    last = n_p + n_s - 1

    def piece_spec(first_tile, n_tiles, lag):
        def index(i, f):
            return (jnp.clip((i + lag - first_tile) * nf + f, 0, n_tiles * nf - 1), 0)
        return pl.BlockSpec((piece, D_MODEL), index)

    def slab_index(i, f):
        return jnp.minimum(i * nf + f, convert_steps - 1)

    assert n_s == 1
    convert_steps = 1 << (((last + 1) * nf).bit_length() - 1)
    next_in, next_out, next_shapes = [], [], []
    for w in next_weights:
        slab_rows = w.shape[1] // convert_steps
        assert slab_rows % BF16_SUBLANE_TILE == 0
        next_in.append(pl.BlockSpec((None, slab_rows, w.shape[2]), lambda i, f: (layer + 1, slab_index(i, f), 0)))
        next_out.append(pl.BlockSpec((None, slab_rows, w.shape[2]), lambda i, f: (0, slab_index(i, f), 0)))
        next_shapes.append(jax.ShapeDtypeStruct((1,) + w.shape[1:], jnp.bfloat16))

    vec = pl.BlockSpec((None, 1, D_MODEL), lambda i, f: (layer, 0, 0))
    chunk = lambda i, f: jnp.where(i > last, nf - 1, f)
    return pl.pallas_call(
        functools.partial(_ffn_kernel, n_p, n_s, len(next_weights)),
        grid=(n_p + n_s + 1, nf),
        in_specs=[
            pl.BlockSpec(memory_space=pl.ANY),
            piece_spec(0, n_p, 1),
            piece_spec(n_p, n_s, 1),
            pl.BlockSpec((None, D_MODEL, tf), lambda i, f: (w_slab, 0, chunk(i, f))),
            pl.BlockSpec((None, tf, D_MODEL), lambda i, f: (w_slab, chunk(i, f), 0)),
            vec, vec, vec, vec,
        ] + next_in,
        out_specs=[piece_spec(0, n_p, -1), piece_spec(n_p, n_s, -1)] + next_out,
        out_shape=[jax.ShapeDtypeStruct(r_p.shape, jnp.float32),
                   jax.ShapeDtypeStruct(r_s.shape, jnp.float32)] + next_shapes,
        scratch_shapes=[pltpu.VMEM((tm, D_MODEL), jnp.float32), pltpu.VMEM((tm, D_MODEL), jnp.float32),
                        pltpu.VMEM((tm, D_MODEL), jnp.bfloat16), pltpu.VMEM((tm, D_MODEL), jnp.bfloat16),
                        pltpu.SemaphoreType.DMA((FIRST_TILE_CHUNKS,))],
        compiler_params=pltpu.CompilerParams(
            dimension_semantics=("arbitrary", "arbitrary"), vmem_limit_bytes=V7X_VMEM_LIMIT_BYTES),
        name="ffn",
    )(r_p, r_p, r_s, w1, w2, g1, b1, g2, b2, *next_weights)


def kernel(x_prompt, x_sample, state_pool, state_conv, w_in, w_pool, pool_scale, ln_v_g, ln_v_b, w_s, b_s, w_conv, w_out, ln1_g, ln1_b, w_ff1, w_ff2, ln2_g, ln2_b):
    bf16 = jnp.bfloat16
    batch, seq, _ = x_prompt.shape
    nseq, dec_len, _ = x_sample.shape
    assert seq % MIX_TILE_ROWS == 0 and MIX_TILE_ROWS % CHUNK == 0
    assert CHUNK % dec_len == 0 and PAST_LEN % CHUNK == 0
    assert (nseq * dec_len) % SAMPLE_TILE_ROWS == 0 and SAMPLE_TILE_ROWS % CHUNK == 0
    assert (batch * seq) % FFN_TILE_ROWS == 0 and (nseq * dec_len) % FFN_TILE_ROWS == 0

    w_pool_b = w_pool.astype(bf16)
    row = lambda a: a.reshape(DEPTH, 1, -1)
    pool_scale, ln_v_g, ln_v_b = row(pool_scale), row(ln_v_g), row(ln_v_b)
    ln1_g, ln1_b, ln2_g, ln2_b = row(ln1_g), row(ln1_b), row(ln2_g), row(ln2_b)

    exact = lax.Precision.HIGHEST
    head_cols = (jnp.arange(GMLP_WIDTH)[None, :] // GMLP_HEAD_DIM == jnp.arange(GMLP_HEADS)[:, None]).astype(jnp.float32)
    pos_rows = (jnp.arange(CHUNK)[:, None] % dec_len == jnp.arange(dec_len)[None, :]).astype(jnp.float32)
    bs_prompt = jnp.einsum('lhi,hc->lic', b_s, head_cols, precision=exact)
    bs_sample = jnp.einsum('ik,lhk,hc->lic', pos_rows, b_s[:, :, :dec_len], head_cols, precision=exact)
    ws_sample = jnp.einsum('ik,lhkm,jm->lhij', pos_rows, w_s[:, :, :dec_len, :dec_len], pos_rows, precision=exact)

    xp = x_prompt.reshape(batch * seq, D_MODEL)
    xs = x_sample.reshape(nseq * dec_len, D_MODEL)
    pool_p, conv_p, p_new, cz_new, chunk_v = [], [], [], [], []
    w_in_b, w_out_b = w_in[:1].astype(bf16), w_out[:1].astype(bf16)
    w1, w2 = w_ff1, w_ff2
    for l in range(DEPTH):
        shared = (w_in_b, w_pool_b, pool_scale, ln_v_g, ln_v_b)
        tail = (w_conv, w_out_b)
        has_next = l + 1 < DEPTH
        rp, p16, cz8, *next_wide = _mixer_prompt(l, 0, xp, shared + (w_s, bs_prompt) + tail, batch, seq,
                                                 (w_in, w_out) if has_next else ())
        rs, p_s, cz_s, v_s = _mixer_sample(l, 0, xs, state_pool, state_conv,
                                           shared + (ws_sample, bs_sample) + tail)
        xp, xs, *next_ffn = _ffn(l, rp, rs, w1, w2, 0, ln1_g, ln1_b, ln2_g, ln2_b,
                                 (w_ff1, w_ff2) if has_next else ())
        if has_next:
            (w_in_b, w_out_b), (w1, w2) = next_wide, next_ffn
        pool_p.append(p16[:, POOL_HIST_PAD - POOL_HIST:])
        conv_p.append(cz8[:, CONV_HIST_PAD - CONV_HIST:])
        p_new.append(p_s.reshape(nseq, dec_len, POOL_WIDTH))
        cz_new.append(cz_s.reshape(nseq, dec_len, CONV_WIDTH))
        chunk_v.append(v_s.reshape(nseq, dec_len, GMLP_WIDTH))

    pool_s = jnp.concatenate([state_pool, jnp.stack(p_new)], axis=2)[:, :, -POOL_HIST:]
    conv_s = jnp.concatenate([state_conv, jnp.stack(cz_new)], axis=2)[:, :, -CONV_HIST:]
    return (xp.reshape(batch, seq, D_MODEL), xs.reshape(nseq, dec_len, D_MODEL),
            jnp.stack(pool_p), jnp.stack(conv_p), pool_s, conv_s, jnp.stack(chunk_v))
```
